```python
import jax, jax.numpy as jnp
from jax import lax
import numpy as np

D_MODEL = 1024
BATCH = 16
SEQ = 2048
DEPTH = 1

M_HEADS = 4
M_V_DIM = 128
M_QK_DIM = 64
M_CHUNK = 64
A_HEADS = 8
A_HEAD_DIM = 64
WINDOWS = (128, 512, 2048)
DILATIONS = (1, 4, 16)
ROT_DIM = A_HEAD_DIM // 4
ROPE_THETA = 500000.0
MIX_WIDTH = M_HEADS * M_V_DIM + A_HEADS * A_HEAD_DIM
IN_SPLITS = (M_HEADS * M_QK_DIM, M_HEADS * M_QK_DIM, M_HEADS * M_V_DIM, M_HEADS * M_V_DIM,
             4 * M_HEADS, A_HEADS * A_HEAD_DIM, A_HEADS * A_HEAD_DIM, A_HEADS * A_HEAD_DIM)
IN_COLS = sum(IN_SPLITS)
D_FF = 2816
N_MOD = 9
HALF_STEP = 0.5
EPS = 1e-6
NEG_INF = -1e30

kernel_name = 'hybrid_mlstm_dilated_attn_macaron_adaln'


def rms_norm(x, g):
    xf = x.astype(jnp.float32)
    y = xf * lax.rsqrt(jnp.mean(xf * xf, axis=-1, keepdims=True) + EPS)
    return (y * g.astype(jnp.float32)).astype(x.dtype)


def modulate(h, shift, scale):
    return h * (1.0 + scale) + shift


def swiglu(h, w_gu, w_down):
    g, u = jnp.split(h @ w_gu, 2, axis=-1)
    return (jax.nn.silu(g) * u) @ w_down


def partial_rope(t, pos):
    half = ROT_DIM // 2
    inv_freq = ROPE_THETA ** (-2.0 * jnp.arange(half, dtype=jnp.float32) / ROT_DIM)
    ang = pos.astype(jnp.float32)[:, None] * inv_freq[None, :]
    cos, sin = jnp.cos(ang), jnp.sin(ang)
    t1, t2 = t[..., :half], t[..., half:ROT_DIM]
    return jnp.concatenate([t1 * cos - t2 * sin, t2 * cos + t1 * sin, t[..., ROT_DIM:]], axis=-1)


def mlstm_scan(q, k, v, log_i, log_f):
    B, H, S, Dk = q.shape
    Dv = v.shape[-1]
    lc = min(M_CHUNK, S)
    nc = S // lc

    def chunks(t):
        return jnp.moveaxis(t.reshape((B, H, nc, lc) + t.shape[3:]), 2, 0)

    xs = (chunks(q), chunks(k), chunks(v), chunks(log_i), chunks(log_f))
    tril = jnp.tril(jnp.ones((lc, lc), dtype=bool))

    def step(carry, inp):
        C, n, m = carry
        q_c, k_c, v_c, li, lf = inp
        b = jnp.cumsum(lf, axis=-1)
        a_inter = b + m[..., None]
        d = jnp.where(tril, b[..., :, None] - b[..., None, :] + li[..., None, :], NEG_INF)
        m_t = jnp.maximum(a_inter, jnp.max(d, axis=-1))
        w_inter = jnp.exp(a_inter - m_t)
        s = jnp.exp(d - m_t[..., None]) * jnp.einsum('bhtd,bhsd->bhts', q_c, k_c)
        num = (w_inter[..., None] * jnp.einsum('bhvd,bhtd->bhtv', C, q_c)
               + jnp.einsum('bhts,bhsv->bhtv', s, v_c))
        den = w_inter * jnp.einsum('bhd,bhtd->bht', n, q_c) + jnp.sum(s, axis=-1)
        h = num / jnp.maximum(jnp.abs(den), jnp.exp(-m_t))[..., None]
        b_end = b[..., -1]
        g = b_end[..., None] - b + li
        m_new = jnp.maximum(b_end + m, jnp.max(g, axis=-1))
        decay = jnp.exp(b_end + m - m_new)
        wg = jnp.exp(g - m_new[..., None])
        C_new = decay[..., None, None] * C + jnp.einsum('bhs,bhsv,bhsd->bhvd', wg, v_c, k_c)
        n_new = decay[..., None] * n + jnp.einsum('bhs,bhsd->bhd', wg, k_c)
        return (C_new, n_new, m_new), h

    init = (jnp.zeros((B, H, Dv, Dk), jnp.float32), jnp.zeros((B, H, Dk), jnp.float32),
            jnp.full((B, H), NEG_INF, jnp.float32))
    _, hs = lax.scan(step, init, xs)
    return jnp.moveaxis(hs, 0, 2).reshape(B, H, S, Dv)


def bidirectional_mlstm(q, k, v, li_f, lf_f, li_b, lf_b):
    h_fwd = mlstm_scan(q, k, v, li_f, lf_f)
    flip = lambda t: jnp.flip(t, axis=2)
    h_bwd = flip(mlstm_scan(flip(q), flip(k), flip(v), flip(li_b), flip(lf_b)))
    return h_fwd + h_bwd


def dilated_band_attention(q, k, v, dilation, n_side):
    B, H, S, Dh = q.shape
    L = S // dilation

    def to_sub(t):
        return t.reshape(B, H, L, dilation, Dh).transpose(0, 1, 3, 2, 4)

    blk = n_side
    nb = -(-L // blk)
    lp = nb * blk
    qs = jnp.pad(to_sub(q), ((0, 0),) * 3 + ((0, lp - L), (0, 0)))
    kpad = ((0, 0),) * 3 + ((n_side, lp - L + n_side), (0, 0))
    ks = jnp.pad(to_sub(k), kpad)
    vs = jnp.pad(to_sub(v), kpad)
    qb = qs.reshape(B, H, dilation, nb, blk, Dh)

    def windows(t):
        tb = t.reshape(B, H, dilation, nb + 2, blk, Dh)
        return jnp.concatenate([tb[:, :, :, :-2], tb[:, :, :, 1:-1], tb[:, :, :, 2:]], axis=4)

    kw, vw = windows(ks), windows(vs)
    s = jnp.einsum('bhrnqe,bhrnke->bhrnqk', qb, kw)
    qi = jnp.arange(blk)[:, None]
    kk = jnp.arange(3 * blk)[None, :]
    band = jnp.abs(kk - n_side - qi) <= n_side
    key_idx = jnp.arange(nb)[:, None] * blk + kk - n_side
    in_range = (key_idx >= 0) & (key_idx < L)
    mask = band[None, :, :] & in_range[:, None, :]
    s = jnp.where(mask, s, NEG_INF)
    m = jnp.max(s, axis=-1, keepdims=True)
    p = jnp.exp(s - m)
    den = jnp.sum(p, axis=-1)
    o = jnp.einsum('bhrnqk,bhrnke->bhrnqe', p, vw) / den[..., None]
    lse = m[..., 0] + jnp.log(den)
    o = o.reshape(B, H, dilation, lp, Dh)[:, :, :, :L].transpose(0, 1, 3, 2, 4).reshape(B, H, S, Dh)
    lse = lse.reshape(B, H, dilation, lp)[..., :L].transpose(0, 1, 3, 2).reshape(B, H, S)
    return o, lse


def dilated_attention(q, k, v):
    outs, lses = [], []
    for w, d in zip(WINDOWS, DILATIONS):
        o, l = dilated_band_attention(q, k, v, d, w // (2 * d))
        outs.append(o)
        lses.append(l)
    wts = jax.nn.softmax(jnp.stack(lses), axis=0)
    return jnp.sum(wts[..., None] * jnp.stack(outs), axis=0)


def hybrid_mixer(h, w_in, gate_bias, g_q, g_k, g_mh, w_out):
    B, S, _ = h.shape
    proj = (h @ w_in).astype(jnp.float32)
    idx, acc = [], 0
    for n_cols in IN_SPLITS[:-1]:
        acc += n_cols
        idx.append(acc)
    mq, mk, mv, mo, mg, aq, ak, av = jnp.split(proj, idx, axis=-1)

    def heads(t, n):
        return t.reshape(B, S, n, -1).transpose(0, 2, 1, 3)

    gates = mg.reshape(B, S, 4, M_HEADS).transpose(2, 0, 3, 1) + gate_bias.astype(jnp.float32)[:, None, :, None]
    hm = bidirectional_mlstm(heads(mq, M_HEADS), heads(mk, M_HEADS) * (M_QK_DIM ** -0.5), heads(mv, M_HEADS),
                             gates[0], jax.nn.log_sigmoid(gates[1]), gates[2], jax.nn.log_sigmoid(gates[3]))
    hm = rms_norm(hm.transpose(0, 2, 1, 3), g_mh.reshape(M_HEADS, M_V_DIM))
    hm = (hm * jax.nn.sigmoid(mo.reshape(B, S, M_HEADS, M_V_DIM))).reshape(B, S, M_HEADS * M_V_DIM)

    pos = jnp.arange(S)
    qa = partial_rope(rms_norm(heads(aq, A_HEADS), g_q), pos) * (A_HEAD_DIM ** -0.5)
    ka = partial_rope(rms_norm(heads(ak, A_HEADS), g_k), pos)
    ha = dilated_attention(qa, ka, heads(av, A_HEADS))
    ha = ha.transpose(0, 2, 1, 3).reshape(B, S, A_HEADS * A_HEAD_DIM)

    return jnp.concatenate([hm, ha], axis=-1).astype(h.dtype) @ w_out


def setup_inputs(seed: int = 0) -> dict:
    key = jax.random.key(seed)
    ks = jax.random.split(key, 20)
    f32 = jnp.float32

    def nrm(k, shape, scale):
        return jax.random.normal(k, shape, f32) * scale

    gate_bias = (nrm(ks[9], (DEPTH, 4, M_HEADS), 0.1)
                 + jnp.array([0.0, 1.0, 0.0, 1.0], f32)[None, :, None]
                 * jnp.linspace(3.0, 6.0, M_HEADS, dtype=f32)[None, None, :])
    return {
        'x': nrm(ks[0], (BATCH, SEQ, D_MODEL), 1.0),
        'c': nrm(ks[1], (BATCH, D_MODEL), 1.0),
        'w_ada': nrm(ks[2], (DEPTH, D_MODEL, N_MOD * D_MODEL), 0.5 * D_MODEL ** -0.5),
        'b_ada': nrm(ks[3], (DEPTH, N_MOD * D_MODEL), 0.02),
        'g_ffn1': 1.0 + nrm(ks[4], (DEPTH, D_MODEL), 0.02),
        'w_gu1': nrm(ks[5], (DEPTH, D_MODEL, 2 * D_FF), D_MODEL ** -0.5),
        'w_down1': nrm(ks[6], (DEPTH, D_FF, D_MODEL), D_FF ** -0.5),
        'g_mix': 1.0 + nrm(ks[7], (DEPTH, D_MODEL), 0.02),
        'w_in': nrm(ks[8], (DEPTH, D_MODEL, IN_COLS), D_MODEL ** -0.5),
        'gate_bias': gate_bias,
        'g_q': 1.0 + nrm(ks[10], (DEPTH, A_HEAD_DIM), 0.02),
        'g_k': 1.0 + nrm(ks[11], (DEPTH, A_HEAD_DIM), 0.02),
        'g_mh': 1.0 + nrm(ks[12], (DEPTH, M_HEADS * M_V_DIM), 0.02),
        'w_out': nrm(ks[13], (DEPTH, MIX_WIDTH, D_MODEL), MIX_WIDTH ** -0.5),
        'g_ffn2': 1.0 + nrm(ks[14], (DEPTH, D_MODEL), 0.02),
        'w_gu2': nrm(ks[15], (DEPTH, D_MODEL, 2 * D_FF), D_MODEL ** -0.5),
        'w_down2': nrm(ks[16], (DEPTH, D_FF, D_MODEL), D_FF ** -0.5),
        'g_final': 1.0 + nrm(ks[17], (DEPTH, D_MODEL), 0.02),
    }


def reference(x, c, w_ada, b_ada, g_ffn1, w_gu1, w_down1, g_mix, w_in, gate_bias, g_q, g_k, g_mh,
              w_out, g_ffn2, w_gu2, w_down2, g_final):
    B, S, D = x.shape
    cs = jax.nn.silu(c)
    for l in range(DEPTH):
        mod = (cs @ w_ada[l] + b_ada[l]).reshape(B, N_MOD, D)[:, :, None, :]
        sh1, sc1, gt1, sh2, sc2, gt2, sh3, sc3, gt3 = [mod[:, i] for i in range(N_MOD)]
        h = modulate(rms_norm(x, g_ffn1[l]), sh1, sc1)
        x = x + HALF_STEP * gt1 * swiglu(h, w_gu1[l], w_down1[l])
        h = modulate(rms_norm(x, g_mix[l]), sh2, sc2)
        x = x + gt2 * hybrid_mixer(h, w_in[l], gate_bias[l], g_q[l], g_k[l], g_mh[l], w_out[l])
        h = modulate(rms_norm(x, g_ffn2[l]), sh3, sc3)
        x = x + HALF_STEP * gt3 * swiglu(h, w_gu2[l], w_down2[l])
        x = rms_norm(x, g_final[l])
    return x
```

```python
import functools

import jax
import jax.numpy as jnp
from jax import lax
from jax.experimental import pallas as pl
from jax.experimental.pallas import tpu as pltpu

F32 = jnp.float32
BF16 = jnp.bfloat16

D_MODEL = 1024
D_FF = 2816
N_MOD = 9
M_HEADS = 4
M_V_DIM = 128
M_QK_DIM = 64
M_CHUNK = 64
A_HEADS = 8
A_HEAD_DIM = 64
WINDOWS = (128, 512, 2048)
DILATIONS = (1, 4, 16)
ROT_DIM = A_HEAD_DIM // 4
ROPE_THETA = 500000.0
EPS = 1e-6
NEG_INF = -1e30
HALF_STEP = 0.5

LANES = 128
M_W = M_HEADS * M_V_DIM
M_QKW = M_HEADS * M_QK_DIM
A_W = A_HEADS * A_HEAD_DIM
N_GATES = 4 * M_HEADS
C_MQ, C_MK, C_MV, C_MO = 0, M_QKW, 2 * M_QKW, 2 * M_QKW + M_W
C_AQ = C_MO + M_W
C_AK = C_AQ + A_W
C_AV = C_AK + A_W
C_G = C_AV + A_W
IN_COLS_PAD = C_G + LANES

VMEM_LIMIT = 56 * 1024 * 1024


def _cparams(n_axes):
    return pltpu.CompilerParams(dimension_semantics=("arbitrary",) * n_axes,
                                vmem_limit_bytes=VMEM_LIMIT)


def _rms(xf, g):
    ms = jnp.mean(xf * xf, axis=-1, keepdims=True)
    return xf * lax.rsqrt(ms + EPS) * g


def _adaln_kernel(c_ref, w_ref, b_ref, o_ref):
    c = c_ref[...]
    cs = (c * jax.nn.sigmoid(c)).astype(BF16)
    o_ref[...] = jnp.dot(cs, w_ref[...].astype(BF16), preferred_element_type=F32) + b_ref[...]


def _adaln(c, w, b):
    bsz, d = c.shape
    n = w.shape[1]
    tn = 1024
    return pl.pallas_call(
        _adaln_kernel,
        grid=(n // tn,),
        in_specs=[pl.BlockSpec((bsz, d), lambda j: (0, 0)),
                  pl.BlockSpec((d, tn), lambda j: (0, j)),
                  pl.BlockSpec((1, tn), lambda j: (0, j))],
        out_specs=pl.BlockSpec((bsz, tn), lambda j: (0, j)),
        out_shape=jax.ShapeDtypeStruct((bsz, n), F32),
        compiler_params=_cparams(1),
        name="adaln",
    )(c, w, b)


def _ffn_kernel(*refs, mod_base, with_mix, final_norm):
    it = iter(refs)
    x_ref = next(it)
    if with_mix:
        hm_ref, ha_ref, wo_m_ref, wo_a_ref = next(it), next(it), next(it), next(it)
    mod_ref, g_ref, wgu_ref, wd_ref = next(it), next(it), next(it), next(it)
    if final_norm:
        gfin_ref = next(it)
    o_ref = next(it)

    x = x_ref[...]
    if with_mix:
        gt_mix = mod_ref[0, mod_base - 1:mod_base, :]
        y = (jnp.dot(hm_ref[...], wo_m_ref[...], preferred_element_type=F32)
             + jnp.dot(ha_ref[...], wo_a_ref[...], preferred_element_type=F32))
        x = x + gt_mix * y
    sh = mod_ref[0, mod_base:mod_base + 1, :]
    sc = mod_ref[0, mod_base + 1:mod_base + 2, :]
    gt = mod_ref[0, mod_base + 2:mod_base + 3, :]
    h = (_rms(x, g_ref[...]) * (1.0 + sc) + sh).astype(BF16)
    gu = jnp.dot(h, wgu_ref[...], preferred_element_type=F32)
    g = gu[:, :D_FF]
    u = gu[:, D_FF:]
    a = (g * jax.nn.sigmoid(g) * u).astype(BF16)
    y = jnp.dot(a, wd_ref[...], preferred_element_type=F32)
    x = x + HALF_STEP * gt * y
    if final_norm:
        x = _rms(x, gfin_ref[...])
    o_ref[...] = x


def _ffn(x2d, mod3, g, wgu, wd, *, seq, mod_base, tm, mix=None, gfin=None):
    t, d = x2d.shape
    with_mix = mix is not None
    final_norm = gfin is not None
    row = lambda i: (i, 0)
    const = lambda i: (0, 0)
    resident = pl.Buffered(1)
    args = [x2d]
    in_specs = [pl.BlockSpec((tm, d), row)]
    if with_mix:
        hm, ha, wo_m, wo_a = mix
        args += [hm, ha, wo_m, wo_a]
        in_specs += [pl.BlockSpec((tm, M_W), row), pl.BlockSpec((tm, A_W), row),
                     pl.BlockSpec((M_W, d), const, pipeline_mode=resident),
                     pl.BlockSpec((A_W, d), const, pipeline_mode=resident)]
    args += [mod3, g, wgu, wd]
    in_specs += [pl.BlockSpec((1, N_MOD, d), lambda i: ((i * tm) // seq, 0, 0)),
                 pl.BlockSpec((1, d), const),
                 pl.BlockSpec((d, 2 * D_FF), const, pipeline_mode=resident),
                 pl.BlockSpec((D_FF, d), const, pipeline_mode=resident)]
    if final_norm:
        args.append(gfin)
        in_specs.append(pl.BlockSpec((1, d), const))
    return pl.pallas_call(
        functools.partial(_ffn_kernel, mod_base=mod_base, with_mix=with_mix, final_norm=final_norm),
        grid=(t // tm,),
        in_specs=in_specs,
        out_specs=pl.BlockSpec((tm, d), row),
        out_shape=jax.ShapeDtypeStruct((t, d), F32),
        compiler_params=_cparams(1),
        name="ffn_mix" if with_mix else "ffn",
    )(*args)


def _group_norm_rope(a, gain, cos_t, sin_a, sin_b, scale):
    lane = lax.broadcasted_iota(jnp.int32, (1, LANES), 1)
    first = lane < A_HEAD_DIM
    outs = []
    for p in range(A_W // LANES):
        blk = a[:, p * LANES:(p + 1) * LANES]
        sq = blk * blk
        s0 = jnp.sum(jnp.where(first, sq, 0.0), axis=-1, keepdims=True)
        s1 = jnp.sum(jnp.where(first, 0.0, sq), axis=-1, keepdims=True)
        r = jnp.where(first, lax.rsqrt(s0 * (1.0 / A_HEAD_DIM) + EPS),
                      lax.rsqrt(s1 * (1.0 / A_HEAD_DIM) + EPS))
        y = blk * r * gain[:, p * LANES:(p + 1) * LANES]
        half = ROT_DIM // 2
        y = (y * cos_t + pltpu.roll(y, LANES - half, 1) * sin_a + pltpu.roll(y, half, 1) * sin_b)
        outs.append((y * scale).astype(BF16))
    return jnp.concatenate(outs, axis=-1)


def _in_proj_kernel(x_ref, mod_ref, g_ref, w_ref, gb_ref, gq_ref, gk_ref, cos_ref, sa_ref, sb_ref,
                    mq_ref, mk_ref, mv_ref, mo_ref, gate_ref, aq_ref, ak_ref, av_ref):
    x = x_ref[...]
    sh = mod_ref[0, 3:4, :]
    sc = mod_ref[0, 4:5, :]
    h = (_rms(x, g_ref[...]) * (1.0 + sc) + sh).astype(BF16)
    proj = jnp.dot(h, w_ref[...], preferred_element_type=F32)

    mq_ref[...] = proj[:, C_MQ:C_MK].astype(BF16)
    mk_ref[...] = (proj[:, C_MK:C_MV] * (M_QK_DIM ** -0.5)).astype(BF16)
    mv_ref[...] = proj[:, C_MV:C_MO].astype(BF16)
    mo_ref[...] = jax.nn.sigmoid(proj[:, C_MO:C_AQ]).astype(BF16)
    av_ref[...] = proj[:, C_AV:C_G].astype(BF16)

    gb = proj[:, C_G:C_G + LANES] + gb_ref[...]
    lane = lax.broadcasted_iota(jnp.int32, (1, LANES), 1)
    is_forget = (lane % (2 * M_HEADS)) >= M_HEADS
    log_sig = jnp.minimum(gb, 0.0) - jnp.log(1.0 + jnp.exp(-jnp.abs(gb)))
    gate_ref[...] = jnp.where(is_forget, log_sig, gb)[:, :N_GATES]

    cos_t, sin_a, sin_b = cos_ref[...], sa_ref[...], sb_ref[...]
    aq_ref[...] = _group_norm_rope(proj[:, C_AQ:C_AK], gq_ref[...], cos_t, sin_a, sin_b,
                                   A_HEAD_DIM ** -0.5)
    ak_ref[...] = _group_norm_rope(proj[:, C_AK:C_AV], gk_ref[...], cos_t, sin_a, sin_b, 1.0)


def _rope_tables(seq):
    half = ROT_DIM // 2
    inv_freq = ROPE_THETA ** (-2.0 * jnp.arange(half, dtype=F32) / ROT_DIM)
    ang = jnp.arange(seq, dtype=F32)[:, None] * inv_freq[None, :]
    cos, sin = jnp.cos(ang), jnp.sin(ang)
    ones = jnp.ones((seq, A_HEAD_DIM - ROT_DIM), F32)
    zeros_h = jnp.zeros((seq, half), F32)
    zeros_r = jnp.zeros((seq, A_HEAD_DIM - ROT_DIM), F32)
    cos_t = jnp.concatenate([cos, cos, ones], axis=-1)
    sin_a = jnp.concatenate([-sin, zeros_h, zeros_r], axis=-1)
    sin_b = jnp.concatenate([zeros_h, sin, zeros_r], axis=-1)
    tile2 = lambda t: jnp.concatenate([t, t], axis=-1)
    return tile2(cos_t), tile2(sin_a), tile2(sin_b)


def _in_proj(x2d, mod3, g, w_cat, gate_bias, gq, gk, tables, *, seq, tm):
    t, d = x2d.shape
    row = lambda i: (i, 0)
    const = lambda i: (0, 0)
    pos = lambda i: ((i * tm) % seq // tm, 0)
    bf = lambda n: jax.ShapeDtypeStruct((t, n), BF16)
    out_shape = (bf(M_QKW), bf(M_QKW), bf(M_W), bf(M_W),
                 jax.ShapeDtypeStruct((t, N_GATES), F32), bf(A_W), bf(A_W), bf(A_W))
    out_specs = tuple(pl.BlockSpec((tm, s.shape[1]), row) for s in out_shape)
    return pl.pallas_call(
        _in_proj_kernel,
        grid=(t // tm,),
        in_specs=[pl.BlockSpec((tm, d), row),
                  pl.BlockSpec((1, N_MOD, d), lambda i: ((i * tm) // seq, 0, 0)),
                  pl.BlockSpec((1, d), const),
                  pl.BlockSpec((d, IN_COLS_PAD), const, pipeline_mode=pl.Buffered(1)),
                  pl.BlockSpec((1, LANES), const),
                  pl.BlockSpec((1, A_W), const),
                  pl.BlockSpec((1, A_W), const),
                  pl.BlockSpec((tm, LANES), pos),
                  pl.BlockSpec((tm, LANES), pos),
                  pl.BlockSpec((tm, LANES), pos)],
        out_specs=out_specs,
        out_shape=out_shape,
        compiler_params=_cparams(1),
        name="in_proj",
    )(x2d, mod3, g, w_cat, gate_bias, gq, gk, *tables)


def _split3(x):
    hi = x.astype(BF16)
    r1 = x - hi.astype(F32)
    mid = r1.astype(BF16)
    lo = (r1 - mid.astype(F32)).astype(BF16)
    return hi, mid, lo


def _tri_left(tri, x):
    hi, mid, lo = _split3(x)
    d = lambda p: jnp.dot(tri, p, preferred_element_type=F32)
    return (d(lo) + d(mid)) + d(hi)


def _tri_right(x, tri):
    hi, mid, lo = _split3(x)
    d = lambda p: jnp.dot(p, tri, preferred_element_type=F32)
    return (d(lo) + d(mid)) + d(hi)


def _mlstm_chain(q, k, v, qk, b_col, u_col, u_row, b_tot, mask, c_ref, n_ref, m_ref, j):
    d = jnp.where(mask, b_col + u_row, NEG_INF)
    m_in = jnp.max(d, axis=-1, keepdims=True)
    s = jnp.exp(d - m_in) * qk
    num_in = jnp.dot(s.astype(BF16), v, preferred_element_type=F32)
    den_in = jnp.sum(s, axis=-1, keepdims=True)

    c_st = c_ref[j]
    n_st = n_ref[j]
    m_st = m_ref[j]
    a = b_col + m_st
    m_t = jnp.maximum(a, m_in)
    w_st = jnp.exp(a - m_t)
    w_in = jnp.exp(m_in - m_t)
    q_c = jnp.dot(q, c_st.astype(BF16), preferred_element_type=F32)
    q_n = jnp.sum(q.astype(F32) * n_st, axis=-1, keepdims=True)
    num = w_st * q_c + w_in * num_in
    den = w_st * q_n + w_in * den_in
    h = num / jnp.maximum(jnp.abs(den), jnp.exp(-m_t))

    max_u = jnp.max(u_row, axis=-1, keepdims=True)
    m_g = b_tot + max_u
    w_g = jnp.exp(u_col - max_u)
    kf = k.astype(F32) * w_g
    upd = lax.dot_general(kf.astype(BF16), v, (((0,), (0,)), ((), ())), preferred_element_type=F32)
    n_upd = jnp.sum(kf, axis=0, keepdims=True)
    m_new = jnp.maximum(b_tot + m_st, m_g)
    alpha = jnp.exp(b_tot + m_st - m_new)
    beta = jnp.exp(m_g - m_new)
    c_ref[j] = alpha * c_st + beta * upd
    n_ref[j] = alpha * n_st + beta * n_upd
    m_ref[j] = m_new
    return h


def _mlstm_kernel(q_ref, k_ref, v_ref, og_ref, gc_ref, gr_ref, gmh_ref, out_ref,
                  hf_ref, hb_ref, c_ref, n_ref, m_ref, *, seq):
    lc = M_CHUNK
    nc = seq // lc
    ri = lax.broadcasted_iota(jnp.int32, (lc, lc), 0)
    ci = lax.broadcasted_iota(jnp.int32, (lc, lc), 1)
    lower = ci <= ri
    upper = ci >= ri
    tril = jnp.where(lower, 1.0, 0.0).astype(BF16)
    triu = jnp.where(upper, 1.0, 0.0).astype(BF16)

    c_ref[...] = jnp.zeros(c_ref.shape, F32)
    n_ref[...] = jnp.zeros(n_ref.shape, F32)
    m_ref[...] = jnp.full(m_ref.shape, NEG_INF, F32)

    def body(c, carry):
        cf = c
        cb = nc - 1 - c
        rf = pl.multiple_of(cf * lc, lc)
        rb = pl.multiple_of(cb * lc, lc)
        gcol_f = gc_ref[0, pl.ds(rf, lc), :]
        gcol_b = gc_ref[0, pl.ds(rb, lc), :]
        grow_f = gr_ref[0, cf]
        grow_b = gr_ref[0, cb]
        bcol_f = _tri_left(tril, gcol_f)
        bcol_b = _tri_left(triu, gcol_b)
        brow_f = _tri_right(grow_f, triu)
        brow_b = _tri_right(grow_b, tril)
        for h in range(M_HEADS):
            ks = slice(h * M_QK_DIM, (h + 1) * M_QK_DIM)
            vs = slice(h * M_V_DIM, (h + 1) * M_V_DIM)
            for direction in range(2):
                if direction == 0:
                    r0, gcol, grow, bcol, brow, mask = rf, gcol_f, grow_f, bcol_f, brow_f, lower
                    gi, gf = h, M_HEADS + h
                    b_tot = bcol[lc - 1:lc, gf:gf + 1]
                else:
                    r0, gcol, grow, bcol, brow, mask = rb, gcol_b, grow_b, bcol_b, brow_b, upper
                    gi, gf = 2 * M_HEADS + h, 3 * M_HEADS + h
                    b_tot = bcol[0:1, gf:gf + 1]
                q = q_ref[0, pl.ds(r0, lc), ks]
                k = k_ref[0, pl.ds(r0, lc), ks]
                v = v_ref[0, pl.ds(r0, lc), vs]
                qk = lax.dot_general(q, k, (((1,), (1,)), ((), ())), preferred_element_type=F32)
                b_col = bcol[:, gf:gf + 1]
                u_col = gcol[:, gi:gi + 1] - b_col
                u_row = grow[gi:gi + 1, :] - brow[gf:gf + 1, :]
                hid = _mlstm_chain(q, k, v, qk, b_col, u_col, u_row, b_tot, mask,
                                   c_ref, n_ref, m_ref, 2 * h + direction)
                if direction == 0:
                    hf_ref[pl.ds(r0, lc), vs] = hid
                else:
                    hb_ref[pl.ds(r0, lc), vs] = hid
        return carry

    lax.fori_loop(0, nc, body, 0)

    for h in range(M_HEADS):
        vs = slice(h * M_V_DIM, (h + 1) * M_V_DIM)
        hsum = hf_ref[:, vs] + hb_ref[:, vs]
        y = _rms(hsum, gmh_ref[:, vs])
        out_ref[0, :, vs] = (y * og_ref[0, :, vs].astype(F32)).astype(BF16)


def _mlstm(mq, mk, mv, mo, gcol, grow, gmh, *, seq):
    bsz = mq.shape[0]
    nc = seq // M_CHUNK
    b3 = lambda b: (b, 0, 0)
    return pl.pallas_call(
        functools.partial(_mlstm_kernel, seq=seq),
        grid=(bsz,),
        in_specs=[pl.BlockSpec((1, seq, M_QKW), b3),
                  pl.BlockSpec((1, seq, M_QKW), b3),
                  pl.BlockSpec((1, seq, M_W), b3),
                  pl.BlockSpec((1, seq, M_W), b3),
                  pl.BlockSpec((1, seq, N_GATES), b3),
                  pl.BlockSpec((1, nc, N_GATES, M_CHUNK), lambda b: (b, 0, 0, 0)),
                  pl.BlockSpec((1, M_W), lambda b: (0, 0))],
        out_specs=pl.BlockSpec((1, seq, M_W), b3),
        out_shape=jax.ShapeDtypeStruct((bsz, seq, M_W), BF16),
        scratch_shapes=[pltpu.VMEM((seq, M_W), F32),
                        pltpu.VMEM((seq, M_W), F32),
                        pltpu.VMEM((2 * M_HEADS, M_QK_DIM, M_V_DIM), F32),
                        pltpu.VMEM((2 * M_HEADS, 1, M_QK_DIM), F32),
                        pltpu.VMEM((2 * M_HEADS, 1, 1), F32)],
        compiler_params=_cparams(1),
        name="mlstm",
    )(mq, mk, mv, mo, gcol, grow, gmh)


ATT_QB = 128


def _band_pattern(q_ref, k_ref, v_ref, o_ref, lse_ref, *, seq, sub_len, n_side):
    kw = min(ATT_QB + 2 * n_side, sub_len)
    lane = lax.broadcasted_iota(jnp.int32, (1, LANES), 1)
    first = lane < A_HEAD_DIM
    qi = lax.broadcasted_iota(jnp.int32, (ATT_QB, kw), 0)
    ki = lax.broadcasted_iota(jnp.int32, (ATT_QB, kw), 1)

    def body(n, carry):
        q0 = pl.multiple_of(n * ATT_QB, ATT_QB)
        seg0 = (q0 // sub_len) * sub_len
        k0 = pl.multiple_of(jnp.clip(q0 - n_side, seg0, seg0 + sub_len - kw), n_side)
        qb = q_ref[pl.ds(q0, ATT_QB), :]
        kb = k_ref[pl.ds(k0, kw), :]
        vb = v_ref[pl.ds(k0, kw), :]
        mask = jnp.abs((ki + k0) - (qi + q0)) <= n_side
        outs, lses = [], []
        for hh in range(2):
            sel = first if hh == 0 else jnp.logical_not(first)
            qm = jnp.where(sel, qb, jnp.zeros_like(qb))
            s = lax.dot_general(qm, kb, (((1,), (1,)), ((), ())), preferred_element_type=F32)
            s = jnp.where(mask, s, NEG_INF)
            m = jnp.max(s, axis=-1, keepdims=True)
            p = jnp.exp(s - m)
            den = jnp.sum(p, axis=-1, keepdims=True)
            o = jnp.dot(p.astype(BF16), vb, preferred_element_type=F32)
            outs.append(o / den)
            lses.append(m + jnp.log(den))
        o_ref[pl.ds(q0, ATT_QB), :] = jnp.where(first, outs[0], outs[1])
        lse_ref[pl.ds(q0, ATT_QB), :] = jnp.where(first, lses[0], lses[1])
        return carry

    lax.fori_loop(0, seq // ATT_QB, body, 0)


def _dil_attn_kernel(q_ref, k_ref, v_ref, out_ref, qf_ref, kf_ref, vf_ref, qp_ref, kp_ref, vp_ref,
                     op_ref, lp_ref, o_ref, l_ref, *, seq):
    n_pat = len(DILATIONS)
    qf_ref[...] = q_ref[0].astype(F32)
    kf_ref[...] = k_ref[0].astype(F32)
    vf_ref[...] = v_ref[0].astype(F32)
    for pi, (w, dil) in enumerate(zip(WINDOWS, DILATIONS)):
        n_side = w // (2 * dil)
        sub_len = seq // dil
        if dil == 1:
            _band_pattern(q_ref.at[0], k_ref.at[0], v_ref.at[0], o_ref.at[pi], l_ref.at[pi],
                          seq=seq, sub_len=sub_len, n_side=n_side)
            continue
        for r in range(dil):
            rows = pl.ds(r * sub_len, sub_len)
            strided = pl.ds(r, sub_len, stride=dil)
            qp_ref[rows, :] = qf_ref[strided, :].astype(BF16)
            kp_ref[rows, :] = kf_ref[strided, :].astype(BF16)
            vp_ref[rows, :] = vf_ref[strided, :].astype(BF16)
        _band_pattern(qp_ref, kp_ref, vp_ref, op_ref, lp_ref,
                      seq=seq, sub_len=sub_len, n_side=n_side)
        for r in range(dil):
            rows = pl.ds(r * sub_len, sub_len)
            strided = pl.ds(r, sub_len, stride=dil)
            o_ref[pi, strided, :] = op_ref[rows, :]
            l_ref[pi, strided, :] = lp_ref[rows, :]
    lse = [l_ref[pi] for pi in range(n_pat)]
    top = functools.reduce(jnp.maximum, lse)
    wts = [jnp.exp(l - top) for l in lse]
    tot = functools.reduce(lambda a, b: a + b, wts)
    acc = functools.reduce(lambda a, b: a + b, [wts[pi] * o_ref[pi] for pi in range(n_pat)])
    out_ref[0] = (acc / tot).astype(BF16)


def _dil_attn(aq, ak, av, *, seq):
    bsz = aq.shape[0]
    n_pairs = A_W // LANES
    blk = pl.BlockSpec((1, seq, LANES), lambda b, p: (b, 0, p))
    f32_slab = pltpu.VMEM((seq, LANES), F32)
    bf_slab = pltpu.VMEM((seq, LANES), BF16)
    return pl.pallas_call(
        functools.partial(_dil_attn_kernel, seq=seq),
        grid=(bsz, n_pairs),
        in_specs=[blk, blk, blk],
        out_specs=blk,
        out_shape=jax.ShapeDtypeStruct((bsz, seq, A_W), BF16),
        scratch_shapes=[f32_slab, f32_slab, f32_slab, bf_slab, bf_slab, bf_slab,
                        f32_slab, f32_slab,
                        pltpu.VMEM((len(DILATIONS), seq, LANES), F32),
                        pltpu.VMEM((len(DILATIONS), seq, LANES), F32)],
        compiler_params=_cparams(2),
        name="dil_attn",
    )(aq, ak, av)


def _layer(x, mod3, g_ffn1, w_gu1, w_down1, g_mix, w_in, gate_bias, g_q, g_k, g_mh, w_out,
           g_ffn2, w_gu2, w_down2, g_final):
    bsz, seq, d = x.shape
    t = bsz * seq
    tm = 512
    row2 = lambda v: v.reshape(1, -1).astype(F32)

    x2d = x.reshape(t, d)
    x1 = _ffn(x2d, mod3, row2(g_ffn1), w_gu1.astype(BF16), w_down1.astype(BF16),
              seq=seq, mod_base=0, tm=tm)

    o_mq, o_mk, o_mv, o_mo = 0, M_QKW, 2 * M_QKW, 2 * M_QKW + M_W
    o_g = o_mo + M_W
    o_aq = o_g + N_GATES
    w_cat = jnp.concatenate(
        [w_in[:, o_mq:o_g], w_in[:, o_aq:], w_in[:, o_g:o_aq],
         jnp.zeros((d, LANES - N_GATES), w_in.dtype)], axis=1).astype(BF16)
    gb = jnp.concatenate([gate_bias.reshape(1, N_GATES).astype(F32),
                          jnp.zeros((1, LANES - N_GATES), F32)], axis=1)
    gq = jnp.tile(row2(g_q), (1, A_HEADS))
    gk = jnp.tile(row2(g_k), (1, A_HEADS))
    mq, mk, mv, mo, gates, aq, ak, av = _in_proj(
        x1, mod3, row2(g_mix), w_cat, gb, gq, gk, _rope_tables(seq), seq=seq, tm=tm)

    to3 = lambda a: a.reshape(bsz, seq, a.shape[-1])
    gcol = to3(gates)
    grow = gcol.reshape(bsz, seq // M_CHUNK, M_CHUNK, N_GATES).transpose(0, 1, 3, 2)
    hm = _mlstm(to3(mq), to3(mk), to3(mv), to3(mo), gcol, grow, row2(g_mh), seq=seq)
    ha = _dil_attn(to3(aq), to3(ak), to3(av), seq=seq)

    w_out_b = w_out.astype(BF16)
    out = _ffn(x1, mod3, row2(g_ffn2), w_gu2.astype(BF16), w_down2.astype(BF16),
               seq=seq, mod_base=6, tm=tm,
               mix=(hm.reshape(t, M_W), ha.reshape(t, A_W), w_out_b[:M_W], w_out_b[M_W:]),
               gfin=row2(g_final))
    return out.reshape(bsz, seq, d)


def kernel(x, c, w_ada, b_ada, g_ffn1, w_gu1, w_down1, g_mix, w_in, gate_bias, g_q, g_k, g_mh,
           w_out, g_ffn2, w_gu2, w_down2, g_final):
    bsz, seq, d = x.shape
    depth = w_ada.shape[0]
    for l in range(depth):
        mod = _adaln(c, w_ada[l], b_ada[l].reshape(1, -1))
        mod3 = mod.reshape(bsz, N_MOD, d)
        x = _layer(x, mod3, g_ffn1[l], w_gu1[l], w_down1[l], g_mix[l], w_in[l], gate_bias[l],
                   g_q[l], g_k[l], g_mh[l], w_out[l], g_ffn2[l], w_gu2[l], w_down2[l], g_final[l])
    return x
```

```python
import functools

import jax
import jax.numpy as jnp
from jax import lax
from jax.experimental import pallas as pl
from jax.experimental.pallas import tpu as pltpu

F32 = jnp.float32
BF16 = jnp.bfloat16

D_MODEL = 1024
D_FF = 2816
N_MOD = 9
M_HEADS = 4
M_V_DIM = 128
M_QK_DIM = 64
M_CHUNK = 64
A_HEADS = 8
A_HEAD_DIM = 64
WINDOWS = (128, 512, 2048)
DILATIONS = (1, 4, 16)
ROT_DIM = A_HEAD_DIM // 4
ROPE_THETA = 500000.0
EPS = 1e-6
NEG_INF = -1e30
HALF_STEP = 0.5

LANES = 128
M_W = M_HEADS * M_V_DIM
M_QKW = M_HEADS * M_QK_DIM
A_W = A_HEADS * A_HEAD_DIM
N_GATES = 4 * M_HEADS
C_MQ, C_MK, C_MV, C_MO = 0, M_QKW, 2 * M_QKW, 2 * M_QKW + M_W
C_AQ = C_MO + M_W
C_AK = C_AQ + A_W
C_AV = C_AK + A_W
C_G = C_AV + A_W
IN_COLS_PAD = C_G + LANES

VMEM_LIMIT = 56 * 1024 * 1024


def _cparams(n_axes):
    return pltpu.CompilerParams(dimension_semantics=("arbitrary",) * n_axes,
                                vmem_limit_bytes=VMEM_LIMIT)


def _aligned(x, m):
    return x if isinstance(x, int) else pl.multiple_of(x, m)


def _rms(xf, g):
    ms = jnp.mean(xf * xf, axis=-1, keepdims=True)
    return xf * lax.rsqrt(ms + EPS) * g


def _adaln_kernel(c_ref, w_ref, b_ref, o_ref):
    c = c_ref[...]
    cs = (c * jax.nn.sigmoid(c)).astype(BF16)
    o_ref[...] = jnp.dot(cs, w_ref[...].astype(BF16), preferred_element_type=F32) + b_ref[...]


def _adaln(c, w, b):
    bsz, d = c.shape
    n = w.shape[1]
    tn = 1024
    return pl.pallas_call(
        _adaln_kernel,
        grid=(n // tn,),
        in_specs=[pl.BlockSpec((bsz, d), lambda j: (0, 0)),
                  pl.BlockSpec((d, tn), lambda j: (0, j)),
                  pl.BlockSpec((1, tn), lambda j: (0, j))],
        out_specs=pl.BlockSpec((bsz, tn), lambda j: (0, j)),
        out_shape=jax.ShapeDtypeStruct((bsz, n), F32),
        compiler_params=_cparams(1),
        name="adaln",
    )(c, w, b)


def _ffn_kernel(*refs, mod_base, with_mix, final_norm):
    it = iter(refs)
    x_ref = next(it)
    if with_mix:
        hm_ref, ha_ref, wo_m_ref, wo_a_ref = next(it), next(it), next(it), next(it)
    mod_ref, g_ref, wgu_ref, wd_ref = next(it), next(it), next(it), next(it)
    if final_norm:
        gfin_ref = next(it)
    o_ref = next(it)

    x = x_ref[...]
    if with_mix:
        gt_mix = mod_ref[0, mod_base - 1:mod_base, :]
        y = (jnp.dot(hm_ref[...], wo_m_ref[...], preferred_element_type=F32)
             + jnp.dot(ha_ref[...], wo_a_ref[...], preferred_element_type=F32))
        x = x + gt_mix * y
    sh = mod_ref[0, mod_base:mod_base + 1, :]
    sc = mod_ref[0, mod_base + 1:mod_base + 2, :]
    gt = mod_ref[0, mod_base + 2:mod_base + 3, :]
    h = (_rms(x, g_ref[...]) * (1.0 + sc) + sh).astype(BF16)
    gu = jnp.dot(h, wgu_ref[...], preferred_element_type=F32)
    g = gu[:, :D_FF]
    u = gu[:, D_FF:]
    a = (g * jax.nn.sigmoid(g) * u).astype(BF16)
    y = jnp.dot(a, wd_ref[...], preferred_element_type=F32)
    x = x + HALF_STEP * gt * y
    if final_norm:
        x = _rms(x, gfin_ref[...])
    o_ref[...] = x


def _ffn(x2d, mod3, g, wgu, wd, *, seq, mod_base, tm, mix=None, gfin=None):
    t, d = x2d.shape
    with_mix = mix is not None
    final_norm = gfin is not None
    row = lambda i: (i, 0)
    const = lambda i: (0, 0)
    resident = pl.Buffered(1)
    args = [x2d]
    in_specs = [pl.BlockSpec((tm, d), row)]
    if with_mix:
        hm, ha, wo_m, wo_a = mix
        args += [hm, ha, wo_m, wo_a]
        in_specs += [pl.BlockSpec((tm, M_W), row), pl.BlockSpec((tm, A_W), row),
                     pl.BlockSpec((M_W, d), const, pipeline_mode=resident),
                     pl.BlockSpec((A_W, d), const, pipeline_mode=resident)]
    args += [mod3, g, wgu, wd]
    in_specs += [pl.BlockSpec((1, N_MOD, d), lambda i: ((i * tm) // seq, 0, 0)),
                 pl.BlockSpec((1, d), const),
                 pl.BlockSpec((d, 2 * D_FF), const, pipeline_mode=resident),
                 pl.BlockSpec((D_FF, d), const, pipeline_mode=resident)]
    if final_norm:
        args.append(gfin)
        in_specs.append(pl.BlockSpec((1, d), const))
    return pl.pallas_call(
        functools.partial(_ffn_kernel, mod_base=mod_base, with_mix=with_mix, final_norm=final_norm),
        grid=(t // tm,),
        in_specs=in_specs,
        out_specs=pl.BlockSpec((tm, d), row),
        out_shape=jax.ShapeDtypeStruct((t, d), F32),
        compiler_params=_cparams(1),
        name="ffn_mix" if with_mix else "ffn",
    )(*args)


def _group_norm_rope(a, gain, cos_t, sin_a, sin_b, scale):
    lane = lax.broadcasted_iota(jnp.int32, (1, LANES), 1)
    first = lane < A_HEAD_DIM
    outs = []
    for p in range(A_W // LANES):
        blk = a[:, p * LANES:(p + 1) * LANES]
        sq = blk * blk
        s0 = jnp.sum(jnp.where(first, sq, 0.0), axis=-1, keepdims=True)
        s1 = jnp.sum(jnp.where(first, 0.0, sq), axis=-1, keepdims=True)
        r = jnp.where(first, lax.rsqrt(s0 * (1.0 / A_HEAD_DIM) + EPS),
                      lax.rsqrt(s1 * (1.0 / A_HEAD_DIM) + EPS))
        y = blk * r * gain[:, p * LANES:(p + 1) * LANES]
        half = ROT_DIM // 2
        y = (y * cos_t + pltpu.roll(y, LANES - half, 1) * sin_a + pltpu.roll(y, half, 1) * sin_b)
        outs.append((y * scale).astype(BF16))
    return jnp.concatenate(outs, axis=-1)


def _in_proj_kernel(x_ref, mod_ref, g_ref, w_ref, gb_ref, gq_ref, gk_ref, cos_ref, sa_ref, sb_ref,
                    mq_ref, mk_ref, mv_ref, mo_ref, gate_ref, aq_ref, ak_ref, av_ref):
    x = x_ref[...]
    sh = mod_ref[0, 3:4, :]
    sc = mod_ref[0, 4:5, :]
    h = (_rms(x, g_ref[...]) * (1.0 + sc) + sh).astype(BF16)
    proj = jnp.dot(h, w_ref[...], preferred_element_type=F32)

    mq_ref[...] = proj[:, C_MQ:C_MK].astype(BF16)
    mk_ref[...] = (proj[:, C_MK:C_MV] * (M_QK_DIM ** -0.5)).astype(BF16)
    mv_ref[...] = proj[:, C_MV:C_MO].astype(BF16)
    mo_ref[...] = jax.nn.sigmoid(proj[:, C_MO:C_AQ]).astype(BF16)
    av_ref[...] = proj[:, C_AV:C_G].astype(BF16)

    gb = proj[:, C_G:C_G + LANES] + gb_ref[...]
    lane = lax.broadcasted_iota(jnp.int32, (1, LANES), 1)
    is_forget = (lane % (2 * M_HEADS)) >= M_HEADS
    log_sig = jnp.minimum(gb, 0.0) - jnp.log(1.0 + jnp.exp(-jnp.abs(gb)))
    gate_ref[...] = jnp.where(is_forget, log_sig, gb)[:, :N_GATES]

    cos_t, sin_a, sin_b = cos_ref[...], sa_ref[...], sb_ref[...]
    aq_ref[...] = _group_norm_rope(proj[:, C_AQ:C_AK], gq_ref[...], cos_t, sin_a, sin_b,
                                   A_HEAD_DIM ** -0.5 * LOG2E)
    ak_ref[...] = _group_norm_rope(proj[:, C_AK:C_AV], gk_ref[...], cos_t, sin_a, sin_b, 1.0)


def _rope_tables(seq):
    half = ROT_DIM // 2
    inv_freq = ROPE_THETA ** (-2.0 * jnp.arange(half, dtype=F32) / ROT_DIM)
    ang = jnp.arange(seq, dtype=F32)[:, None] * inv_freq[None, :]
    cos, sin = jnp.cos(ang), jnp.sin(ang)
    ones = jnp.ones((seq, A_HEAD_DIM - ROT_DIM), F32)
    zeros_h = jnp.zeros((seq, half), F32)
    zeros_r = jnp.zeros((seq, A_HEAD_DIM - ROT_DIM), F32)
    cos_t = jnp.concatenate([cos, cos, ones], axis=-1)
    sin_a = jnp.concatenate([-sin, zeros_h, zeros_r], axis=-1)
    sin_b = jnp.concatenate([zeros_h, sin, zeros_r], axis=-1)
    tile2 = lambda t: jnp.concatenate([t, t], axis=-1)
    return tile2(cos_t), tile2(sin_a), tile2(sin_b)


def _in_proj(x2d, mod3, g, w_cat, gate_bias, gq, gk, tables, *, seq, tm):
    t, d = x2d.shape
    row = lambda i: (i, 0)
    const = lambda i: (0, 0)
    pos = lambda i: ((i * tm) % seq // tm, 0)
    bf = lambda n: jax.ShapeDtypeStruct((t, n), BF16)
    out_shape = (bf(M_QKW), bf(M_QKW), bf(M_W), bf(M_W),
                 jax.ShapeDtypeStruct((t, N_GATES), F32), bf(A_W), bf(A_W), bf(A_W))
    out_specs = tuple(pl.BlockSpec((tm, s.shape[1]), row) for s in out_shape)
    return pl.pallas_call(
        _in_proj_kernel,
        grid=(t // tm,),
        in_specs=[pl.BlockSpec((tm, d), row),
                  pl.BlockSpec((1, N_MOD, d), lambda i: ((i * tm) // seq, 0, 0)),
                  pl.BlockSpec((1, d), const),
                  pl.BlockSpec((d, IN_COLS_PAD), const, pipeline_mode=pl.Buffered(1)),
                  pl.BlockSpec((1, LANES), const),
                  pl.BlockSpec((1, A_W), const),
                  pl.BlockSpec((1, A_W), const),
                  pl.BlockSpec((tm, LANES), pos),
                  pl.BlockSpec((tm, LANES), pos),
                  pl.BlockSpec((tm, LANES), pos)],
        out_specs=out_specs,
        out_shape=out_shape,
        compiler_params=_cparams(1),
        name="in_proj",
    )(x2d, mod3, g, w_cat, gate_bias, gq, gk, *tables)


def _split3(x):
    hi = x.astype(BF16)
    r1 = x - hi.astype(F32)
    mid = r1.astype(BF16)
    lo = (r1 - mid.astype(F32)).astype(BF16)
    return hi, mid, lo


def _tri_left(tri, x):
    hi, mid, lo = _split3(x)
    d = lambda p: jnp.dot(tri, p, preferred_element_type=F32)
    return (d(lo) + d(mid)) + d(hi)


def _tri_right(x, tri):
    hi, mid, lo = _split3(x)
    d = lambda p: jnp.dot(p, tri, preferred_element_type=F32)
    return (d(lo) + d(mid)) + d(hi)


def _mlstm_chain(q, k, v, qk, b_col, u_col, u_row, b_tot, mask, c_ref, n_ref, m_ref, j):
    d = jnp.where(mask, b_col + u_row, NEG_INF)
    m_in = jnp.max(d, axis=-1, keepdims=True)
    s = jnp.exp(d - m_in) * qk
    num_in = jnp.dot(s.astype(BF16), v, preferred_element_type=F32)
    den_in = jnp.sum(s, axis=-1, keepdims=True)

    c_st = c_ref[j]
    n_st = n_ref[j]
    m_st = m_ref[j]
    a = b_col + m_st
    m_t = jnp.maximum(a, m_in)
    w_st = jnp.exp(a - m_t)
    w_in = jnp.exp(m_in - m_t)
    q_c = jnp.dot(q, c_st.astype(BF16), preferred_element_type=F32)
    q_n = jnp.sum(q.astype(F32) * n_st, axis=-1, keepdims=True)
    num = w_st * q_c + w_in * num_in
    den = w_st * q_n + w_in * den_in
    h = num / jnp.maximum(jnp.abs(den), jnp.exp(-m_t))

    max_u = jnp.max(u_row, axis=-1, keepdims=True)
    m_g = b_tot + max_u
    w_g = jnp.exp(u_col - max_u)
    kf = k.astype(F32) * w_g
    upd = lax.dot_general(kf.astype(BF16), v, (((0,), (0,)), ((), ())), preferred_element_type=F32)
    n_upd = jnp.sum(kf, axis=0, keepdims=True)
    m_new = jnp.maximum(b_tot + m_st, m_g)
    alpha = jnp.exp(b_tot + m_st - m_new)
    beta = jnp.exp(m_g - m_new)
    c_ref[j] = alpha * c_st + beta * upd
    n_ref[j] = alpha * n_st + beta * n_upd
    m_ref[j] = m_new
    return h


def _mlstm_kernel(q_ref, k_ref, v_ref, og_ref, gc_ref, gr_ref, gmh_ref, out_ref,
                  hf_ref, hb_ref, c_ref, n_ref, m_ref, *, seq):
    lc = M_CHUNK
    nc = seq // lc
    ri = lax.broadcasted_iota(jnp.int32, (lc, lc), 0)
    ci = lax.broadcasted_iota(jnp.int32, (lc, lc), 1)
    lower = ci <= ri
    upper = ci >= ri
    tril = jnp.where(lower, 1.0, 0.0).astype(BF16)
    triu = jnp.where(upper, 1.0, 0.0).astype(BF16)

    c_ref[...] = jnp.zeros(c_ref.shape, F32)
    n_ref[...] = jnp.zeros(n_ref.shape, F32)
    m_ref[...] = jnp.full(m_ref.shape, NEG_INF, F32)

    def body(c, carry):
        cf = c
        cb = nc - 1 - c
        rf = pl.multiple_of(cf * lc, lc)
        rb = pl.multiple_of(cb * lc, lc)
        gcol_f = gc_ref[0, pl.ds(rf, lc), :]
        gcol_b = gc_ref[0, pl.ds(rb, lc), :]
        grow_f = gr_ref[0, cf]
        grow_b = gr_ref[0, cb]
        bcol_f = _tri_left(tril, gcol_f)
        bcol_b = _tri_left(triu, gcol_b)
        brow_f = _tri_right(grow_f, triu)
        brow_b = _tri_right(grow_b, tril)
        for h in range(M_HEADS):
            ks = slice(h * M_QK_DIM, (h + 1) * M_QK_DIM)
            vs = slice(h * M_V_DIM, (h + 1) * M_V_DIM)
            for direction in range(2):
                if direction == 0:
                    r0, gcol, grow, bcol, brow, mask = rf, gcol_f, grow_f, bcol_f, brow_f, lower
                    gi, gf = h, M_HEADS + h
                    b_tot = bcol[lc - 1:lc, gf:gf + 1]
                else:
                    r0, gcol, grow, bcol, brow, mask = rb, gcol_b, grow_b, bcol_b, brow_b, upper
                    gi, gf = 2 * M_HEADS + h, 3 * M_HEADS + h
                    b_tot = bcol[0:1, gf:gf + 1]
                q = q_ref[0, pl.ds(r0, lc), ks]
                k = k_ref[0, pl.ds(r0, lc), ks]
                v = v_ref[0, pl.ds(r0, lc), vs]
                qk = lax.dot_general(q, k, (((1,), (1,)), ((), ())), preferred_element_type=F32)
                b_col = bcol[:, gf:gf + 1]
                u_col = gcol[:, gi:gi + 1] - b_col
                u_row = grow[gi:gi + 1, :] - brow[gf:gf + 1, :]
                hid = _mlstm_chain(q, k, v, qk, b_col, u_col, u_row, b_tot, mask,
                                   c_ref, n_ref, m_ref, 2 * h + direction)
                if direction == 0:
                    hf_ref[pl.ds(r0, lc), vs] = hid
                else:
                    hb_ref[pl.ds(r0, lc), vs] = hid
        return carry

    lax.fori_loop(0, nc, body, 0)

    for h in range(M_HEADS):
        vs = slice(h * M_V_DIM, (h + 1) * M_V_DIM)
        hsum = hf_ref[:, vs] + hb_ref[:, vs]
        y = _rms(hsum, gmh_ref[:, vs])
        out_ref[0, :, vs] = (y * og_ref[0, :, vs].astype(F32)).astype(BF16)


def _mlstm(mq, mk, mv, mo, gcol, grow, gmh, *, seq):
    bsz = mq.shape[0]
    nc = seq // M_CHUNK
    b3 = lambda b: (b, 0, 0)
    return pl.pallas_call(
        functools.partial(_mlstm_kernel, seq=seq),
        grid=(bsz,),
        in_specs=[pl.BlockSpec((1, seq, M_QKW), b3),
                  pl.BlockSpec((1, seq, M_QKW), b3),
                  pl.BlockSpec((1, seq, M_W), b3),
                  pl.BlockSpec((1, seq, M_W), b3),
                  pl.BlockSpec((1, seq, N_GATES), b3),
                  pl.BlockSpec((1, nc, N_GATES, M_CHUNK), lambda b: (b, 0, 0, 0)),
                  pl.BlockSpec((1, M_W), lambda b: (0, 0))],
        out_specs=pl.BlockSpec((1, seq, M_W), b3),
        out_shape=jax.ShapeDtypeStruct((bsz, seq, M_W), BF16),
        scratch_shapes=[pltpu.VMEM((seq, M_W), F32),
                        pltpu.VMEM((seq, M_W), F32),
                        pltpu.VMEM((2 * M_HEADS, M_QK_DIM, M_V_DIM), F32),
                        pltpu.VMEM((2 * M_HEADS, 1, M_QK_DIM), F32),
                        pltpu.VMEM((2 * M_HEADS, 1, 1), F32)],
        compiler_params=_cparams(1),
        name="mlstm",
    )(mq, mk, mv, mo, gcol, grow, gmh)


ATT_QB = 128
ATT_SIDE = 64
ATT_KW = ATT_QB + 2 * ATT_SIDE
LOG2E = 1.4426950408889634
assert all(w // (2 * d) == ATT_SIDE for w, d in zip(WINDOWS, DILATIONS))
assert DILATIONS == (1, 4, 16)


def _band_bias(kw, offset):
    qi = lax.broadcasted_iota(jnp.int32, (2 * ATT_QB, kw), 0) % ATT_QB
    ki = lax.broadcasted_iota(jnp.int32, (2 * ATT_QB, kw), 1)
    return jnp.where(jnp.abs(ki + offset - qi) <= ATT_SIDE, 0.0, NEG_INF).astype(F32)


def _attn_block(qb, kb, vb, bias):
    first = lax.broadcasted_iota(jnp.int32, (1, LANES), 1) < A_HEAD_DIM
    zero = jnp.zeros_like(qb)
    qm = jnp.concatenate([jnp.where(first, qb, zero), jnp.where(first, zero, qb)], axis=0)
    s = lax.dot_general(qm, kb, (((1,), (1,)), ((), ())), preferred_element_type=F32) + bias
    m = jnp.max(s, axis=-1, keepdims=True)
    p = jnp.exp2(s - m)
    den = jnp.sum(p, axis=-1, keepdims=True)
    o = jnp.dot(p.astype(BF16), vb, preferred_element_type=F32) / den
    lse = m + jnp.log2(den)
    return (jnp.where(first, o[:ATT_QB], o[ATT_QB:]),
            jnp.where(first, lse[:ATT_QB], lse[ATT_QB:]))


def _merge(o_a, l_a, o_b, l_b):
    top = jnp.maximum(l_a, l_b)
    w_a = jnp.exp2(l_a - top)
    w_b = jnp.exp2(l_b - top)
    tot = w_a + w_b
    return (w_a * o_a + w_b * o_b) / tot, top + jnp.log2(tot)


def _dil_attn_kernel(q_ref, k_ref, v_ref, out_ref, tok_ref, r4f_ref, r4_ref, r16_ref,
                     o4_ref, l4_ref, ot_ref, lt_ref, bias_ref, *, seq):
    l4 = seq // 4
    l16 = seq // 16
    n_in = 3

    @pl.when((pl.program_id(0) == 0) & (pl.program_id(1) == 0))
    def _():
        bias_ref[0] = _band_bias(ATT_KW, 0)
        bias_ref[1] = _band_bias(ATT_KW, -ATT_SIDE)
        bias_ref[2] = _band_bias(ATT_KW, ATT_QB - ATT_KW)
        bias_ref[3, :, :ATT_QB] = _band_bias(ATT_QB, 0)

    for i, ref in enumerate((q_ref, k_ref, v_ref)):
        tok_ref[i] = ref[0].astype(F32)
    for i in range(n_in):
        for r in range(4):
            x = tok_ref[i, pl.ds(r, l4, stride=4), :]
            r4f_ref[i, r * l4:(r + 1) * l4, :] = x
            r4_ref[i, r * l4:(r + 1) * l4, :] = x.astype(BF16)
    for i in range(n_in):
        for c in range(16):
            x = r4f_ref[i, pl.ds((c // 4) * l4 + c % 4, l16, stride=4), :]
            r16_ref[i, c * l16:(c + 1) * l16, :] = x.astype(BF16)

    bias3 = bias_ref[3, :, :ATT_QB]

    def pat3(r4, carry):
        for r2 in range(4):
            q0 = _aligned((r4 * 4 + r2) * l16, l16)
            rows = pl.ds(q0, l16)
            o, l = _attn_block(r16_ref[0, rows, :], r16_ref[1, rows, :], r16_ref[2, rows, :], bias3)
            dst = pl.ds(r4 * l4 + r2, l16, stride=4)
            o4_ref[dst, :] = o
            l4_ref[dst, :] = l
        return carry

    lax.fori_loop(0, 4, pat3, 0, unroll=2)

    def window(ref, i, q0, jb, nb):
        off = 0 if jb == 0 else (ATT_QB - ATT_KW if jb == nb - 1 else -ATT_SIDE)
        return ref[i, pl.ds(_aligned(q0 + off, ATT_SIDE), ATT_KW), :]

    def bias_of(jb, nb):
        return bias_ref[0 if jb == 0 else (2 if jb == nb - 1 else 1)]

    nb4 = l4 // ATT_QB

    def pat2(r, carry):
        for jb in range(nb4):
            q0 = _aligned(r * l4 + jb * ATT_QB, ATT_QB)
            rows = pl.ds(q0, ATT_QB)
            o, l = _attn_block(r4_ref[0, rows, :], window(r4_ref, 1, q0, jb, nb4),
                               window(r4_ref, 2, q0, jb, nb4), bias_of(jb, nb4))
            o, l = _merge(o, l, o4_ref[rows, :], l4_ref[rows, :])
            dst = pl.ds(4 * jb * ATT_QB + r, ATT_QB, stride=4)
            ot_ref[dst, :] = o
            lt_ref[dst, :] = l
        return carry

    lax.fori_loop(0, 4, pat2, 0, unroll=2)

    nb1 = seq // ATT_QB

    def pat1_block(n, jb):
        q0 = _aligned(n * ATT_QB, ATT_QB)
        rows = pl.ds(q0, ATT_QB)
        o, l = _attn_block(q_ref[0, rows, :], window(k_ref, 0, q0, jb, nb1),
                           window(v_ref, 0, q0, jb, nb1), bias_of(jb, nb1))
        o, _ = _merge(o, l, ot_ref[rows, :], lt_ref[rows, :])
        out_ref[0, rows, :] = o.astype(BF16)

    per_iter = (nb1 - 2) // 2

    def pat1(g, carry):
        for u in range(per_iter):
            pat1_block(per_iter * g + 1 + u, 1)
        return carry

    pat1_block(0, 0)
    lax.fori_loop(0, 2, pat1, 0)
    for n in range(2 * per_iter + 1, nb1):
        pat1_block(n, n)


def _dil_attn(aq, ak, av, *, seq):
    bsz = aq.shape[0]
    n_pairs = A_W // LANES
    blk = pl.BlockSpec((1, seq, LANES), lambda b, p: (b, 0, p))
    f32_slab = pltpu.VMEM((seq, LANES), F32)
    return pl.pallas_call(
        functools.partial(_dil_attn_kernel, seq=seq),
        grid=(bsz, n_pairs),
        in_specs=[blk, blk, blk],
        out_specs=blk,
        out_shape=jax.ShapeDtypeStruct((bsz, seq, A_W), BF16),
        scratch_shapes=[pltpu.VMEM((3, seq, LANES), F32),
                        pltpu.VMEM((3, seq, LANES), F32),
                        pltpu.VMEM((3, seq, LANES), BF16),
                        pltpu.VMEM((3, seq, LANES), BF16),
                        f32_slab, f32_slab, f32_slab, f32_slab,
                        pltpu.VMEM((4, 2 * ATT_QB, ATT_KW), F32)],
        compiler_params=_cparams(2),
        name="dil_attn",
    )(aq, ak, av)


def _layer(x, mod3, g_ffn1, w_gu1, w_down1, g_mix, w_in, gate_bias, g_q, g_k, g_mh, w_out,
           g_ffn2, w_gu2, w_down2, g_final):
    bsz, seq, d = x.shape
    t = bsz * seq
    tm = 512
    row2 = lambda v: v.reshape(1, -1).astype(F32)

    x2d = x.reshape(t, d)
    x1 = _ffn(x2d, mod3, row2(g_ffn1), w_gu1.astype(BF16), w_down1.astype(BF16),
              seq=seq, mod_base=0, tm=tm)

    o_mq, o_mk, o_mv, o_mo = 0, M_QKW, 2 * M_QKW, 2 * M_QKW + M_W
    o_g = o_mo + M_W
    o_aq = o_g + N_GATES
    w_cat = jnp.concatenate(
        [w_in[:, o_mq:o_g], w_in[:, o_aq:], w_in[:, o_g:o_aq],
         jnp.zeros((d, LANES - N_GATES), w_in.dtype)], axis=1).astype(BF16)
    gb = jnp.concatenate([gate_bias.reshape(1, N_GATES).astype(F32),
                          jnp.zeros((1, LANES - N_GATES), F32)], axis=1)
    gq = jnp.tile(row2(g_q), (1, A_HEADS))
    gk = jnp.tile(row2(g_k), (1, A_HEADS))
    mq, mk, mv, mo, gates, aq, ak, av = _in_proj(
        x1, mod3, row2(g_mix), w_cat, gb, gq, gk, _rope_tables(seq), seq=seq, tm=tm)

    to3 = lambda a: a.reshape(bsz, seq, a.shape[-1])
    gcol = to3(gates)
    grow = gcol.reshape(bsz, seq // M_CHUNK, M_CHUNK, N_GATES).transpose(0, 1, 3, 2)
    hm = _mlstm(to3(mq), to3(mk), to3(mv), to3(mo), gcol, grow, row2(g_mh), seq=seq)
    ha = _dil_attn(to3(aq), to3(ak), to3(av), seq=seq)

    w_out_b = w_out.astype(BF16)
    out = _ffn(x1, mod3, row2(g_ffn2), w_gu2.astype(BF16), w_down2.astype(BF16),
               seq=seq, mod_base=6, tm=tm,
               mix=(hm.reshape(t, M_W), ha.reshape(t, A_W), w_out_b[:M_W], w_out_b[M_W:]),
               gfin=row2(g_final))
    return out.reshape(bsz, seq, d)


def kernel(x, c, w_ada, b_ada, g_ffn1, w_gu1, w_down1, g_mix, w_in, gate_bias, g_q, g_k, g_mh,
           w_out, g_ffn2, w_gu2, w_down2, g_final):
    bsz, seq, d = x.shape
    depth = w_ada.shape[0]
    for l in range(depth):
        mod = _adaln(c, w_ada[l], b_ada[l].reshape(1, -1))
        mod3 = mod.reshape(bsz, N_MOD, d)
        x = _layer(x, mod3, g_ffn1[l], w_gu1[l], w_down1[l], g_mix[l], w_in[l], gate_bias[l],
                   g_q[l], g_k[l], g_mh[l], w_out[l], g_ffn2[l], w_gu2[l], w_down2[l], g_final[l])
    return x
```

```python
import functools

import jax
import jax.numpy as jnp
from jax import lax
from jax.experimental import pallas as pl
from jax.experimental.pallas import tpu as pltpu

F32 = jnp.float32
BF16 = jnp.bfloat16

D_MODEL = 1024
D_FF = 2816
N_MOD = 9
M_HEADS = 4
M_V_DIM = 128
M_QK_DIM = 64
M_CHUNK = 64
A_HEADS = 8
A_HEAD_DIM = 64
WINDOWS = (128, 512, 2048)
DILATIONS = (1, 4, 16)
ROT_DIM = A_HEAD_DIM // 4
ROPE_THETA = 500000.0
EPS = 1e-6
NEG_INF = -1e30
HALF_STEP = 0.5

LANES = 128
M_W = M_HEADS * M_V_DIM
M_QKW = M_HEADS * M_QK_DIM
A_W = A_HEADS * A_HEAD_DIM
N_GATES = 4 * M_HEADS
M_QKD = 2 * M_QKW
C_MQ, C_MK, C_MV, C_MO = 0, M_QKD, 2 * M_QKD, 2 * M_QKD + M_W
C_AQ = C_MO + M_W
C_AK = C_AQ + A_W
C_AV = C_AK + A_W
C_G = C_AV + A_W
IN_COLS_PAD = C_G + LANES

VMEM_LIMIT = 56 * 1024 * 1024


def _cparams(n_axes):
    return pltpu.CompilerParams(dimension_semantics=("arbitrary",) * n_axes,
                                vmem_limit_bytes=VMEM_LIMIT)


def _aligned(x, m):
    return x if isinstance(x, int) else pl.multiple_of(x, m)


def _rms(xf, g):
    ms = jnp.mean(xf * xf, axis=-1, keepdims=True)
    return xf * lax.rsqrt(ms + EPS) * g


def _adaln_kernel(c_ref, w_ref, b_ref, o_ref):
    c = c_ref[...]
    cs = (c * jax.nn.sigmoid(c)).astype(BF16)
    o_ref[...] = jnp.dot(cs, w_ref[...].astype(BF16), preferred_element_type=F32) + b_ref[...]


def _adaln(c, w, b):
    bsz, d = c.shape
    n = w.shape[1]
    tn = 1024
    return pl.pallas_call(
        _adaln_kernel,
        grid=(n // tn,),
        in_specs=[pl.BlockSpec((bsz, d), lambda j: (0, 0)),
                  pl.BlockSpec((d, tn), lambda j: (0, j)),
                  pl.BlockSpec((1, tn), lambda j: (0, j))],
        out_specs=pl.BlockSpec((bsz, tn), lambda j: (0, j)),
        out_shape=jax.ShapeDtypeStruct((bsz, n), F32),
        compiler_params=_cparams(1),
        name="adaln",
    )(c, w, b)


def _ffn_kernel(*refs, mod_base, with_mix, final_norm):
    it = iter(refs)
    x_ref = next(it)
    if with_mix:
        hm_ref, ha_ref, wo_m_ref, wo_a_ref = next(it), next(it), next(it), next(it)
    mod_ref, g_ref, wgu_ref, wd_ref = next(it), next(it), next(it), next(it)
    if final_norm:
        gfin_ref = next(it)
    o_ref = next(it)

    x = x_ref[...]
    if with_mix:
        gt_mix = mod_ref[0, mod_base - 1:mod_base, :]
        y = (jnp.dot(hm_ref[...], wo_m_ref[...], preferred_element_type=F32)
             + jnp.dot(ha_ref[...], wo_a_ref[...], preferred_element_type=F32))
        x = x + gt_mix * y
    sh = mod_ref[0, mod_base:mod_base + 1, :]
    sc = mod_ref[0, mod_base + 1:mod_base + 2, :]
    gt = mod_ref[0, mod_base + 2:mod_base + 3, :]
    h = (_rms(x, g_ref[...]) * (1.0 + sc) + sh).astype(BF16)
    gu = jnp.dot(h, wgu_ref[...], preferred_element_type=F32)
    g = gu[:, :D_FF]
    u = gu[:, D_FF:]
    a = (g * jax.nn.sigmoid(g) * u).astype(BF16)
    y = jnp.dot(a, wd_ref[...], preferred_element_type=F32)
    x = x + HALF_STEP * gt * y
    if final_norm:
        x = _rms(x, gfin_ref[...])
    o_ref[...] = x


def _ffn(x2d, mod3, g, wgu, wd, *, seq, mod_base, tm, mix=None, gfin=None):
    t, d = x2d.shape
    with_mix = mix is not None
    final_norm = gfin is not None
    row = lambda i: (i, 0)
    const = lambda i: (0, 0)
    resident = pl.Buffered(1)
    args = [x2d]
    in_specs = [pl.BlockSpec((tm, d), row)]
    if with_mix:
        hm, ha, wo_m, wo_a = mix
        args += [hm, ha, wo_m, wo_a]
        in_specs += [pl.BlockSpec((tm, M_W), row), pl.BlockSpec((tm, A_W), row),
                     pl.BlockSpec((M_W, d), const, pipeline_mode=resident),
                     pl.BlockSpec((A_W, d), const, pipeline_mode=resident)]
    args += [mod3, g, wgu, wd]
    in_specs += [pl.BlockSpec((1, N_MOD, d), lambda i: ((i * tm) // seq, 0, 0)),
                 pl.BlockSpec((1, d), const),
                 pl.BlockSpec((d, 2 * D_FF), const, pipeline_mode=resident),
                 pl.BlockSpec((D_FF, d), const, pipeline_mode=resident)]
    if final_norm:
        args.append(gfin)
        in_specs.append(pl.BlockSpec((1, d), const))
    return pl.pallas_call(
        functools.partial(_ffn_kernel, mod_base=mod_base, with_mix=with_mix, final_norm=final_norm),
        grid=(t // tm,),
        in_specs=in_specs,
        out_specs=pl.BlockSpec((tm, d), row),
        out_shape=jax.ShapeDtypeStruct((t, d), F32),
        compiler_params=_cparams(1),
        name="ffn_mix" if with_mix else "ffn",
    )(*args)


def _group_norm_rope(a, gain, cos_t, sin_a, sin_b, scale):
    lane = lax.broadcasted_iota(jnp.int32, (1, LANES), 1)
    first = lane < A_HEAD_DIM
    outs = []
    for p in range(A_W // LANES):
        blk = a[:, p * LANES:(p + 1) * LANES]
        sq = blk * blk
        s0 = jnp.sum(jnp.where(first, sq, 0.0), axis=-1, keepdims=True)
        s1 = jnp.sum(jnp.where(first, 0.0, sq), axis=-1, keepdims=True)
        r = jnp.where(first, lax.rsqrt(s0 * (1.0 / A_HEAD_DIM) + EPS),
                      lax.rsqrt(s1 * (1.0 / A_HEAD_DIM) + EPS))
        y = blk * r * gain[:, p * LANES:(p + 1) * LANES]
        half = ROT_DIM // 2
        y = (y * cos_t + pltpu.roll(y, LANES - half, 1) * sin_a + pltpu.roll(y, half, 1) * sin_b)
        outs.append((y * scale).astype(BF16))
    return jnp.concatenate(outs, axis=-1)


def _in_proj_kernel(x_ref, mod_ref, g_ref, w_ref, gb_ref, gq_ref, gk_ref, cos_ref, sa_ref, sb_ref,
                    mq_ref, mk_ref, mv_ref, mo_ref, gate_ref, aq_ref, ak_ref, av_ref):
    x = x_ref[...]
    sh = mod_ref[0, 3:4, :]
    sc = mod_ref[0, 4:5, :]
    h = (_rms(x, g_ref[...]) * (1.0 + sc) + sh).astype(BF16)
    proj = jnp.dot(h, w_ref[...], preferred_element_type=F32)

    mq_ref[...] = proj[:, C_MQ:C_MK].astype(BF16)
    mk_ref[...] = (proj[:, C_MK:C_MV] * (M_QK_DIM ** -0.5)).astype(BF16)
    mv_ref[...] = proj[:, C_MV:C_MO].astype(BF16)
    mo_ref[...] = jax.nn.sigmoid(proj[:, C_MO:C_AQ]).astype(BF16)
    av_ref[...] = proj[:, C_AV:C_G].astype(BF16)

    gb = proj[:, C_G:C_G + LANES] + gb_ref[...]
    lane = lax.broadcasted_iota(jnp.int32, (1, LANES), 1)
    is_forget = (lane % (2 * M_HEADS)) >= M_HEADS
    log_sig = jnp.minimum(gb, 0.0) - jnp.log(1.0 + jnp.exp(-jnp.abs(gb)))
    gate_ref[...] = jnp.where(is_forget, log_sig, gb)[:, :N_GATES]

    cos_t, sin_a, sin_b = cos_ref[...], sa_ref[...], sb_ref[...]
    aq_ref[...] = _group_norm_rope(proj[:, C_AQ:C_AK], gq_ref[...], cos_t, sin_a, sin_b,
                                   A_HEAD_DIM ** -0.5 * LOG2E)
    ak_ref[...] = _group_norm_rope(proj[:, C_AK:C_AV], gk_ref[...], cos_t, sin_a, sin_b, 1.0)


def _rope_tables(seq):
    half = ROT_DIM // 2
    inv_freq = ROPE_THETA ** (-2.0 * jnp.arange(half, dtype=F32) / ROT_DIM)
    ang = jnp.arange(seq, dtype=F32)[:, None] * inv_freq[None, :]
    cos, sin = jnp.cos(ang), jnp.sin(ang)
    ones = jnp.ones((seq, A_HEAD_DIM - ROT_DIM), F32)
    zeros_h = jnp.zeros((seq, half), F32)
    zeros_r = jnp.zeros((seq, A_HEAD_DIM - ROT_DIM), F32)
    cos_t = jnp.concatenate([cos, cos, ones], axis=-1)
    sin_a = jnp.concatenate([-sin, zeros_h, zeros_r], axis=-1)
    sin_b = jnp.concatenate([zeros_h, sin, zeros_r], axis=-1)
    tile2 = lambda t: jnp.concatenate([t, t], axis=-1)
    return tile2(cos_t), tile2(sin_a), tile2(sin_b)


def _in_proj(x2d, mod3, g, w_cat, gate_bias, gq, gk, tables, *, seq, tm):
    t, d = x2d.shape
    row = lambda i: (i, 0)
    const = lambda i: (0, 0)
    pos = lambda i: ((i * tm) % seq // tm, 0)
    bf = lambda n: jax.ShapeDtypeStruct((t, n), BF16)
    out_shape = (bf(M_QKD), bf(M_QKD), bf(M_W), bf(M_W),
                 jax.ShapeDtypeStruct((t, N_GATES), F32), bf(A_W), bf(A_W), bf(A_W))
    out_specs = tuple(pl.BlockSpec((tm, s.shape[1]), row) for s in out_shape)
    return pl.pallas_call(
        _in_proj_kernel,
        grid=(t // tm,),
        in_specs=[pl.BlockSpec((tm, d), row),
                  pl.BlockSpec((1, N_MOD, d), lambda i: ((i * tm) // seq, 0, 0)),
                  pl.BlockSpec((1, d), const),
                  pl.BlockSpec((d, IN_COLS_PAD), const, pipeline_mode=pl.Buffered(1)),
                  pl.BlockSpec((1, LANES), const),
                  pl.BlockSpec((1, A_W), const),
                  pl.BlockSpec((1, A_W), const),
                  pl.BlockSpec((tm, LANES), pos),
                  pl.BlockSpec((tm, LANES), pos),
                  pl.BlockSpec((tm, LANES), pos)],
        out_specs=out_specs,
        out_shape=out_shape,
        compiler_params=_cparams(1),
        name="in_proj",
    )(x2d, mod3, g, w_cat, gate_bias, gq, gk, *tables)


def _split3(x):
    hi = x.astype(BF16)
    r1 = x - hi.astype(F32)
    mid = r1.astype(BF16)
    lo = (r1 - mid.astype(F32)).astype(BF16)
    return hi, mid, lo


M_TILE = 16
C_ROWS = M_V_DIM + M_TILE
R_ALPHA, R_BETA, R_WI, R_EMT, R_WG = range(5)


def _seg_scan(x, op, fill):
    lane = lax.broadcasted_iota(jnp.int32, (1, LANES), 1)
    first = lane < M_CHUNK
    pos = lane % M_CHUNK
    k = 1
    while k < M_CHUNK:
        from_left = jnp.where(pos >= k, pltpu.roll(x, k, 1), fill)
        from_right = jnp.where(pos < M_CHUNK - k, pltpu.roll(x, LANES - k, 1), fill)
        x = op(x, jnp.where(first, from_left, from_right))
        k *= 2
    return x


def _half_reduce(x, red, fill):
    first = lax.broadcasted_iota(jnp.int32, (1, LANES), 1) < M_CHUNK
    a = red(jnp.where(first, x, fill), axis=-1, keepdims=True)
    b = red(jnp.where(first, fill, x), axis=-1, keepdims=True)
    return jnp.where(first, a, b)


def _mlstm_kernel(q_ref, k_ref, v_ref, og_ref, gi_ref, gf_ref, gmh_ref, out_ref,
                  bt_ref, mg_ref, ms_ref, mn_ref, rows_ref, ab_ref, bb_ref, cst_ref, cs_ref, *, seq):
    lc = M_CHUNK
    nc = seq // lc
    nrow = M_HEADS * nc
    dn_t = (((0,), (0,)), ((), ()))
    dn_nt = (((1,), (1,)), ((), ()))
    lane = lax.broadcasted_iota(jnp.int32, (1, LANES), 1)
    first = lane < lc

    gi = gi_ref[0]
    gf = gf_ref[0]
    b2 = _seg_scan(gf, jnp.add, 0.0)
    u2 = gi - b2
    cmax2 = _seg_scan(u2, jnp.maximum, NEG_INF)
    btot2 = _half_reduce(gf, jnp.sum, 0.0)
    maxu2 = _half_reduce(u2, jnp.max, NEG_INF)
    mg2 = btot2 + maxu2
    bt_ref[...] = btot2
    mg_ref[...] = mg2

    m = jnp.full((M_HEADS, LANES), NEG_INF, F32)
    for i in range(nc):
        rows_f = pl.ds(i, M_HEADS, stride=nc)
        rows_b = pl.ds(nc - 1 - i, M_HEADS, stride=nc)
        bt = jnp.where(first, bt_ref[rows_f, :], bt_ref[rows_b, :])
        mg = jnp.where(first, mg_ref[rows_f, :], mg_ref[rows_b, :])
        m_new = jnp.maximum(bt + m, mg)
        ms_ref[0, rows_f, :] = m
        ms_ref[1, rows_b, :] = m
        mn_ref[0, rows_f, :] = m_new
        mn_ref[1, rows_b, :] = m_new
        m = m_new
    mstart2 = jnp.where(first, ms_ref[0], ms_ref[1])
    mnext2 = jnp.where(first, mn_ref[0], mn_ref[1])

    mm2 = jnp.maximum(mstart2, cmax2)
    rows_ref[R_ALPHA] = jnp.exp(btot2 + mstart2 - mnext2)
    rows_ref[R_BETA] = jnp.exp(mg2 - mnext2)
    rows_ref[R_WI] = jnp.exp(mstart2 - mm2)
    rows_ref[R_EMT] = jnp.exp(-(b2 + mm2))
    rows_ref[R_WG] = jnp.exp(u2 - maxu2)

    ab_ref[...] = jnp.zeros(ab_ref.shape, F32)
    bb_ref[...] = jnp.zeros(bb_ref.shape, F32)
    ones = jnp.ones((nrow, LANES), F32)
    for r, (up, mp) in enumerate(zip(_split3(u2), _split3(mm2))):
        ab_ref[pl.ds(r, nrow, stride=M_TILE), :] = up.astype(F32)
        ab_ref[pl.ds(3 + r, nrow, stride=M_TILE), :] = ones
        bb_ref[pl.ds(r, nrow, stride=M_TILE), :] = ones
        bb_ref[pl.ds(3 + r, nrow, stride=M_TILE), :] = -mp.astype(F32)

    cst_ref[...] = jnp.zeros(cst_ref.shape, F32)
    eye = (lax.broadcasted_iota(jnp.int32, (LANES, LANES), 0)
           == lax.broadcasted_iota(jnp.int32, (LANES, LANES), 1))

    def step(i, carry):
        cf = i
        cb = nc - 1 - i
        rf = _aligned(cf * lc, lc)
        rb = _aligned(cb * lc, lc)
        for h in range(M_HEADS):
            vs = slice(h * M_V_DIM, (h + 1) * M_V_DIM)
            st = cst_ref[h]
            st_b = st.astype(BF16)
            cs_ref[h, cf, :, 0:lc] = st_b[:, 0:lc]
            cs_ref[h, cb, :, lc:LANES] = st_b[:, lc:LANES]
            row_f = h * nc + cf
            row_b = h * nc + cb
            pick = lambda j: jnp.where(first, rows_ref[j, pl.ds(row_f, 1), :],
                                       rows_ref[j, pl.ds(row_b, 1), :])
            alpha, beta, wg = pick(R_ALPHA), pick(R_BETA), pick(R_WG)
            k_f = k_ref[0, pl.ds(rf, lc), vs]
            k_b = k_ref[0, pl.ds(rb, lc), vs]
            zero = jnp.zeros_like(k_f)
            k_bd = jnp.concatenate([jnp.where(first, k_f, zero), jnp.where(first, zero, k_b)], axis=0)
            diag = jnp.where(eye, wg, 0.0).astype(BF16)
            kw = jnp.dot(diag, k_bd, preferred_element_type=F32)
            v_st = jnp.concatenate([v_ref[0, pl.ds(rf, lc), vs], v_ref[0, pl.ds(rb, lc), vs]], axis=0)
            upd = lax.dot_general(v_st, kw.astype(BF16), dn_t, preferred_element_type=F32)
            n_upd = jnp.sum(kw, axis=0, keepdims=True)
            cst_ref[h, 0:M_V_DIM, :] = alpha * st[0:M_V_DIM] + beta * upd
            cst_ref[h, M_V_DIM:M_V_DIM + 1, :] = alpha * st[M_V_DIM:M_V_DIM + 1] + beta * n_upd
        return carry

    lax.fori_loop(0, nc, step, 0, unroll=2)

    s_i = lax.broadcasted_iota(jnp.int32, (lc, LANES), 0)
    t_i = lax.broadcasted_iota(jnp.int32, (lc, LANES), 1) % lc
    causal = jnp.where((s_i - t_i) * jnp.where(first, 1, -1) <= 0, 0.0, NEG_INF).astype(F32)

    def chunk(c, carry):
        r0 = _aligned(c * lc, lc)
        for h in range(M_HEADS):
            vs = slice(h * M_V_DIM, (h + 1) * M_V_DIM)
            hc = h * nc + c
            k_d = k_ref[0, pl.ds(r0, lc), vs]
            q_d = q_ref[0, pl.ds(r0, lc), vs]
            zero = jnp.zeros_like(q_d)
            q_bd = jnp.concatenate([jnp.where(first, q_d, zero), jnp.where(first, zero, q_d)], axis=0)
            qk_t = lax.dot_general(k_d, q_bd, dn_nt, preferred_element_type=F32)
            tile = pl.ds(_aligned(hc * M_TILE, M_TILE), M_TILE)
            diff = lax.dot_general(ab_ref[tile, :].astype(BF16), bb_ref[tile, :].astype(BF16), dn_t,
                                   preferred_element_type=F32)
            dm = jnp.where(first, diff[0:lc], diff[lc:]) + causal
            s_t = jnp.exp(dm) * qk_t
            den_in = jnp.sum(s_t, axis=0, keepdims=True)
            num = lax.dot_general(v_ref[0, pl.ds(r0, lc), vs], s_t.astype(BF16), dn_t,
                                  preferred_element_type=F32)
            inter = lax.dot_general(cs_ref[h, c], q_bd, dn_nt, preferred_element_type=F32)
            wi = rows_ref[R_WI, pl.ds(hc, 1), :]
            emt = rows_ref[R_EMT, pl.ds(hc, 1), :]
            num = num + wi * inter[0:M_V_DIM]
            den = den_in + wi * inter[M_V_DIM:M_V_DIM + 1]
            h_t = num * (1.0 / jnp.maximum(jnp.abs(den), emt))
            h_n = h_t.T
            hs = h_n[0:lc] + h_n[lc:]
            y = _rms(hs, gmh_ref[:, vs]) * og_ref[0, pl.ds(r0, lc), vs].astype(F32)
            out_ref[0, pl.ds(r0, lc), vs] = y.astype(BF16)
        return carry

    lax.fori_loop(0, nc, chunk, 0, unroll=4)


def _mlstm(mqd, mkd, mv, mo, gi2, gf2, gmh, *, seq):
    bsz = mqd.shape[0]
    nc = seq // M_CHUNK
    nrow = M_HEADS * nc
    assert 2 * M_CHUNK == LANES and 2 * M_QK_DIM == LANES and M_V_DIM == LANES
    b3 = lambda b: (b, 0, 0)
    slab = pltpu.VMEM((nrow, LANES), F32)
    return pl.pallas_call(
        functools.partial(_mlstm_kernel, seq=seq),
        grid=(bsz,),
        in_specs=[pl.BlockSpec((1, seq, M_QKD), b3),
                  pl.BlockSpec((1, seq, M_QKD), b3),
                  pl.BlockSpec((1, seq, M_W), b3),
                  pl.BlockSpec((1, seq, M_W), b3),
                  pl.BlockSpec((1, nrow, LANES), b3),
                  pl.BlockSpec((1, nrow, LANES), b3),
                  pl.BlockSpec((1, M_W), lambda b: (0, 0))],
        out_specs=pl.BlockSpec((1, seq, M_W), b3),
        out_shape=jax.ShapeDtypeStruct((bsz, seq, M_W), BF16),
        scratch_shapes=[slab, slab,
                        pltpu.VMEM((2, nrow, LANES), F32),
                        pltpu.VMEM((2, nrow, LANES), F32),
                        pltpu.VMEM((5, nrow, LANES), F32),
                        pltpu.VMEM((nrow * M_TILE, LANES), F32),
                        pltpu.VMEM((nrow * M_TILE, LANES), F32),
                        pltpu.VMEM((M_HEADS, C_ROWS, LANES), F32),
                        pltpu.VMEM((M_HEADS, nc, C_ROWS, LANES), BF16)],
        compiler_params=_cparams(1),
        name="mlstm",
    )(mqd, mkd, mv, mo, gi2, gf2, gmh)


ATT_QB = 128
ATT_SIDE = 64
ATT_KW = ATT_QB + 2 * ATT_SIDE
LOG2E = 1.4426950408889634
assert all(w // (2 * d) == ATT_SIDE for w, d in zip(WINDOWS, DILATIONS))
assert DILATIONS == (1, 4, 16)


def _band_bias(kw, offset):
    qi = lax.broadcasted_iota(jnp.int32, (2 * ATT_QB, kw), 0) % ATT_QB
    ki = lax.broadcasted_iota(jnp.int32, (2 * ATT_QB, kw), 1)
    return jnp.where(jnp.abs(ki + offset - qi) <= ATT_SIDE, 0.0, NEG_INF).astype(F32)


def _attn_block(qb, kb, vb, bias):
    first = lax.broadcasted_iota(jnp.int32, (1, LANES), 1) < A_HEAD_DIM
    zero = jnp.zeros_like(qb)
    qm = jnp.concatenate([jnp.where(first, qb, zero), jnp.where(first, zero, qb)], axis=0)
    s = lax.dot_general(qm, kb, (((1,), (1,)), ((), ())), preferred_element_type=F32) + bias
    m = jnp.max(s, axis=-1, keepdims=True)
    p = jnp.exp2(s - m)
    den = jnp.sum(p, axis=-1, keepdims=True)
    o = jnp.dot(p.astype(BF16), vb, preferred_element_type=F32) / den
    lse = m + jnp.log2(den)
    return (jnp.where(first, o[:ATT_QB], o[ATT_QB:]),
            jnp.where(first, lse[:ATT_QB], lse[ATT_QB:]))


def _merge(o_a, l_a, o_b, l_b):
    top = jnp.maximum(l_a, l_b)
    w_a = jnp.exp2(l_a - top)
    w_b = jnp.exp2(l_b - top)
    tot = w_a + w_b
    return (w_a * o_a + w_b * o_b) / tot, top + jnp.log2(tot)


def _dil_attn_kernel(q_ref, k_ref, v_ref, out_ref, tok_ref, r4f_ref, r4_ref, r16_ref,
                     o4_ref, l4_ref, ot_ref, lt_ref, bias_ref, *, seq):
    l4 = seq // 4
    l16 = seq // 16
    n_in = 3

    @pl.when((pl.program_id(0) == 0) & (pl.program_id(1) == 0))
    def _():
        bias_ref[0] = _band_bias(ATT_KW, 0)
        bias_ref[1] = _band_bias(ATT_KW, -ATT_SIDE)
        bias_ref[2] = _band_bias(ATT_KW, ATT_QB - ATT_KW)
        bias_ref[3, :, :ATT_QB] = _band_bias(ATT_QB, 0)

    for i, ref in enumerate((q_ref, k_ref, v_ref)):
        tok_ref[i] = ref[0].astype(F32)
    for i in range(n_in):
        for r in range(4):
            x = tok_ref[i, pl.ds(r, l4, stride=4), :]
            r4f_ref[i, r * l4:(r + 1) * l4, :] = x
            r4_ref[i, r * l4:(r + 1) * l4, :] = x.astype(BF16)
    for i in range(n_in):
        for c in range(16):
            x = r4f_ref[i, pl.ds((c // 4) * l4 + c % 4, l16, stride=4), :]
            r16_ref[i, c * l16:(c + 1) * l16, :] = x.astype(BF16)

    bias3 = bias_ref[3, :, :ATT_QB]

    def pat3(r4, carry):
        for r2 in range(4):
            q0 = _aligned((r4 * 4 + r2) * l16, l16)
            rows = pl.ds(q0, l16)
            o, l = _attn_block(r16_ref[0, rows, :], r16_ref[1, rows, :], r16_ref[2, rows, :], bias3)
            dst = pl.ds(r4 * l4 + r2, l16, stride=4)
            o4_ref[dst, :] = o
            l4_ref[dst, :] = l
        return carry

    lax.fori_loop(0, 4, pat3, 0, unroll=2)

    def window(ref, i, q0, jb, nb):
        off = 0 if jb == 0 else (ATT_QB - ATT_KW if jb == nb - 1 else -ATT_SIDE)
        return ref[i, pl.ds(_aligned(q0 + off, ATT_SIDE), ATT_KW), :]

    def bias_of(jb, nb):
        return bias_ref[0 if jb == 0 else (2 if jb == nb - 1 else 1)]

    nb4 = l4 // ATT_QB

    def pat2(r, carry):
        for jb in range(nb4):
            q0 = _aligned(r * l4 + jb * ATT_QB, ATT_QB)
            rows = pl.ds(q0, ATT_QB)
            o, l = _attn_block(r4_ref[0, rows, :], window(r4_ref, 1, q0, jb, nb4),
                               window(r4_ref, 2, q0, jb, nb4), bias_of(jb, nb4))
            o, l = _merge(o, l, o4_ref[rows, :], l4_ref[rows, :])
            dst = pl.ds(4 * jb * ATT_QB + r, ATT_QB, stride=4)
            ot_ref[dst, :] = o
            lt_ref[dst, :] = l
        return carry

    lax.fori_loop(0, 4, pat2, 0, unroll=2)

    nb1 = seq // ATT_QB

    def pat1_block(n, jb):
        q0 = _aligned(n * ATT_QB, ATT_QB)
        rows = pl.ds(q0, ATT_QB)
        o, l = _attn_block(q_ref[0, rows, :], window(k_ref, 0, q0, jb, nb1),
                           window(v_ref, 0, q0, jb, nb1), bias_of(jb, nb1))
        o, _ = _merge(o, l, ot_ref[rows, :], lt_ref[rows, :])
        out_ref[0, rows, :] = o.astype(BF16)

    per_iter = (nb1 - 2) // 2

    def pat1(g, carry):
        for u in range(per_iter):
            pat1_block(per_iter * g + 1 + u, 1)
        return carry

    pat1_block(0, 0)
    lax.fori_loop(0, 2, pat1, 0)
    for n in range(2 * per_iter + 1, nb1):
        pat1_block(n, n)


def _dil_attn(aq, ak, av, *, seq):
    bsz = aq.shape[0]
    n_pairs = A_W // LANES
    blk = pl.BlockSpec((1, seq, LANES), lambda b, p: (b, 0, p))
    f32_slab = pltpu.VMEM((seq, LANES), F32)
    return pl.pallas_call(
        functools.partial(_dil_attn_kernel, seq=seq),
        grid=(bsz, n_pairs),
        in_specs=[blk, blk, blk],
        out_specs=blk,
        out_shape=jax.ShapeDtypeStruct((bsz, seq, A_W), BF16),
        scratch_shapes=[pltpu.VMEM((3, seq, LANES), F32),
                        pltpu.VMEM((3, seq, LANES), F32),
                        pltpu.VMEM((3, seq, LANES), BF16),
                        pltpu.VMEM((3, seq, LANES), BF16),
                        f32_slab, f32_slab, f32_slab, f32_slab,
                        pltpu.VMEM((4, 2 * ATT_QB, ATT_KW), F32)],
        compiler_params=_cparams(2),
        name="dil_attn",
    )(aq, ak, av)


def _layer(x, mod3, g_ffn1, w_gu1, w_down1, g_mix, w_in, gate_bias, g_q, g_k, g_mh, w_out,
           g_ffn2, w_gu2, w_down2, g_final):
    bsz, seq, d = x.shape
    t = bsz * seq
    tm = 512
    row2 = lambda v: v.reshape(1, -1).astype(F32)

    x2d = x.reshape(t, d)
    x1 = _ffn(x2d, mod3, row2(g_ffn1), w_gu1.astype(BF16), w_down1.astype(BF16),
              seq=seq, mod_base=0, tm=tm)

    o_mk, o_mv = M_QKW, 2 * M_QKW
    o_g = o_mv + 2 * M_W
    o_aq = o_g + N_GATES

    def twice_per_head(w):
        wh = w.reshape(d, M_HEADS, 1, M_QK_DIM)
        return jnp.broadcast_to(wh, (d, M_HEADS, 2, M_QK_DIM)).reshape(d, M_QKD)

    w_cat = jnp.concatenate(
        [twice_per_head(w_in[:, :o_mk]), twice_per_head(w_in[:, o_mk:o_mv]), w_in[:, o_mv:o_g],
         w_in[:, o_aq:], w_in[:, o_g:o_aq],
         jnp.zeros((d, LANES - N_GATES), w_in.dtype)], axis=1).astype(BF16)
    gb = jnp.concatenate([gate_bias.reshape(1, N_GATES).astype(F32),
                          jnp.zeros((1, LANES - N_GATES), F32)], axis=1)
    gq = jnp.tile(row2(g_q), (1, A_HEADS))
    gk = jnp.tile(row2(g_k), (1, A_HEADS))
    mq, mk, mv, mo, gates, aq, ak, av = _in_proj(
        x1, mod3, row2(g_mix), w_cat, gb, gq, gk, _rope_tables(seq), seq=seq, tm=tm)

    to3 = lambda a: a.reshape(bsz, seq, a.shape[-1])
    nc = seq // M_CHUNK
    g5 = gates.reshape(bsz, nc, M_CHUNK, 2, 2, M_HEADS)
    g5 = g5.transpose(4, 0, 5, 1, 3, 2).reshape(2, bsz, M_HEADS * nc, 2 * M_CHUNK)
    hm = _mlstm(to3(mq), to3(mk), to3(mv), to3(mo), g5[0], g5[1], row2(g_mh), seq=seq)
    ha = _dil_attn(to3(aq), to3(ak), to3(av), seq=seq)

    w_out_b = w_out.astype(BF16)
    out = _ffn(x1, mod3, row2(g_ffn2), w_gu2.astype(BF16), w_down2.astype(BF16),
               seq=seq, mod_base=6, tm=tm,
               mix=(hm.reshape(t, M_W), ha.reshape(t, A_W), w_out_b[:M_W], w_out_b[M_W:]),
               gfin=row2(g_final))
    return out.reshape(bsz, seq, d)


def kernel(x, c, w_ada, b_ada, g_ffn1, w_gu1, w_down1, g_mix, w_in, gate_bias, g_q, g_k, g_mh,
           w_out, g_ffn2, w_gu2, w_down2, g_final):
    bsz, seq, d = x.shape
    depth = w_ada.shape[0]
    for l in range(depth):
        mod = _adaln(c, w_ada[l], b_ada[l].reshape(1, -1))
        mod3 = mod.reshape(bsz, N_MOD, d)
        x = _layer(x, mod3, g_ffn1[l], w_gu1[l], w_down1[l], g_mix[l], w_in[l], gate_bias[l],
                   g_q[l], g_k[l], g_mh[l], w_out[l], g_ffn2[l], w_gu2[l], w_down2[l], g_final[l])
    return x
```

```python
import functools

import jax
import jax.numpy as jnp
from jax import lax
from jax.experimental import pallas as pl
from jax.experimental.pallas import tpu as pltpu

F32 = jnp.float32
BF16 = jnp.bfloat16

D_MODEL = 1024
D_FF = 2816
N_MOD = 9
M_HEADS = 4
M_V_DIM = 128
M_QK_DIM = 64
M_CHUNK = 64
A_HEADS = 8
A_HEAD_DIM = 64
WINDOWS = (128, 512, 2048)
DILATIONS = (1, 4, 16)
ROT_DIM = A_HEAD_DIM // 4
ROPE_THETA = 500000.0
EPS = 1e-6
NEG_INF = -1e30
HALF_STEP = 0.5

LANES = 128
M_W = M_HEADS * M_V_DIM
M_QKW = M_HEADS * M_QK_DIM
A_W = A_HEADS * A_HEAD_DIM
N_GATES = 4 * M_HEADS
M_QKD = 2 * M_QKW
C_MQ, C_MK, C_MV, C_MO = 0, M_QKD, 2 * M_QKD, 2 * M_QKD + M_W
C_AQ = C_MO + M_W
C_AK = C_AQ + A_W
C_AV = C_AK + A_W
C_G = C_AV + A_W
IN_COLS_PAD = C_G + LANES

VMEM_LIMIT = 56 * 1024 * 1024


def _cparams(n_axes):
    return pltpu.CompilerParams(dimension_semantics=("arbitrary",) * n_axes,
                                vmem_limit_bytes=VMEM_LIMIT)


def _aligned(x, m):
    return x if isinstance(x, int) else pl.multiple_of(x, m)


def _rms(xf, g):
    ms = jnp.mean(xf * xf, axis=-1, keepdims=True)
    return xf * lax.rsqrt(ms + EPS) * g


def _adaln_kernel(c_ref, w_ref, b_ref, o_ref):
    c = c_ref[...]
    cs = (c * jax.nn.sigmoid(c)).astype(BF16)
    o_ref[...] = jnp.dot(cs, w_ref[...].astype(BF16), preferred_element_type=F32) + b_ref[...]


def _adaln(c, w, b):
    bsz, d = c.shape
    n = w.shape[1]
    tn = 1024
    return pl.pallas_call(
        _adaln_kernel,
        grid=(n // tn,),
        in_specs=[pl.BlockSpec((bsz, d), lambda j: (0, 0)),
                  pl.BlockSpec((d, tn), lambda j: (0, j)),
                  pl.BlockSpec((1, tn), lambda j: (0, j))],
        out_specs=pl.BlockSpec((bsz, tn), lambda j: (0, j)),
        out_shape=jax.ShapeDtypeStruct((bsz, n), F32),
        compiler_params=_cparams(1),
        name="adaln",
    )(c, w, b)


N_PROJ_IN = 8
N_PROJ_OUT = 8


def _ffn_kernel(*refs, mod_base, with_mix, final_norm, with_proj):
    it = iter(refs)
    x_ref = next(it)
    if with_mix:
        hm_ref, ha_ref, wo_m_ref, wo_a_ref = next(it), next(it), next(it), next(it)
    mod_ref, g_ref, wgu_ref, wd_ref = next(it), next(it), next(it), next(it)
    if final_norm:
        gfin_ref = next(it)
    if with_proj:
        proj_in = [next(it) for _ in range(N_PROJ_IN)]
    o_ref = next(it)
    if with_proj:
        proj_out = [next(it) for _ in range(N_PROJ_OUT)]

    x = x_ref[...]
    if with_mix:
        gt_mix = mod_ref[0, mod_base - 1:mod_base, :]
        y = (jnp.dot(hm_ref[...], wo_m_ref[...], preferred_element_type=F32)
             + jnp.dot(ha_ref[...], wo_a_ref[...], preferred_element_type=F32))
        x = x + gt_mix * y
    sh = mod_ref[0, mod_base:mod_base + 1, :]
    sc = mod_ref[0, mod_base + 1:mod_base + 2, :]
    gt = mod_ref[0, mod_base + 2:mod_base + 3, :]
    h = (_rms(x, g_ref[...]) * (1.0 + sc) + sh).astype(BF16)
    gu = jnp.dot(h, wgu_ref[...], preferred_element_type=F32)
    g = gu[:, :D_FF]
    u = gu[:, D_FF:]
    a = (g * jax.nn.sigmoid(g) * u).astype(BF16)
    y = jnp.dot(a, wd_ref[...], preferred_element_type=F32)
    x = x + HALF_STEP * gt * y
    if final_norm:
        x = _rms(x, gfin_ref[...])
    o_ref[...] = x
    if with_proj:
        _mixer_in_proj(x, mod_ref, *proj_in, *proj_out)


def _ffn(x2d, mod3, g, wgu, wd, *, seq, mod_base, tm, mix=None, gfin=None, proj=None):
    t, d = x2d.shape
    with_mix = mix is not None
    final_norm = gfin is not None
    with_proj = proj is not None
    row = lambda i: (i, 0)
    const = lambda i: (0, 0)
    resident = pl.Buffered(1)
    args = [x2d]
    in_specs = [pl.BlockSpec((tm, d), row)]
    if with_mix:
        hm, ha, wo_m, wo_a = mix
        args += [hm, ha, wo_m, wo_a]
        in_specs += [pl.BlockSpec((tm, M_W), row), pl.BlockSpec((tm, A_W), row),
                     pl.BlockSpec((M_W, d), const, pipeline_mode=resident),
                     pl.BlockSpec((A_W, d), const, pipeline_mode=resident)]
    args += [mod3, g, wgu, wd]
    in_specs += [pl.BlockSpec((1, N_MOD, d), lambda i: ((i * tm) // seq, 0, 0)),
                 pl.BlockSpec((1, d), const),
                 pl.BlockSpec((d, 2 * D_FF), const, pipeline_mode=resident),
                 pl.BlockSpec((D_FF, d), const, pipeline_mode=resident)]
    if final_norm:
        args.append(gfin)
        in_specs.append(pl.BlockSpec((1, d), const))
    out_shape = [jax.ShapeDtypeStruct((t, d), F32)]
    if with_proj:
        g_mix, w_cat, gate_bias, gq, gk, tables = proj
        pos = lambda i: ((i * tm) % seq // tm, 0)
        args += [g_mix, w_cat, gate_bias, gq, gk, *tables]
        in_specs += [pl.BlockSpec((1, d), const),
                     pl.BlockSpec((d, IN_COLS_PAD), const, pipeline_mode=resident),
                     pl.BlockSpec((1, LANES), const),
                     pl.BlockSpec((1, A_W), const),
                     pl.BlockSpec((1, A_W), const),
                     pl.BlockSpec((tm, LANES), pos),
                     pl.BlockSpec((tm, LANES), pos),
                     pl.BlockSpec((tm, LANES), pos)]
        bf = lambda n: jax.ShapeDtypeStruct((t, n), BF16)
        out_shape += [bf(M_QKD), bf(M_QKD), bf(M_W), bf(M_W),
                      jax.ShapeDtypeStruct((t, N_GATES), F32), bf(A_W), bf(A_W), bf(A_W)]
    out_specs = [pl.BlockSpec((tm, s.shape[1]), row) for s in out_shape]
    outs = pl.pallas_call(
        functools.partial(_ffn_kernel, mod_base=mod_base, with_mix=with_mix, final_norm=final_norm,
                          with_proj=with_proj),
        grid=(t // tm,),
        in_specs=in_specs,
        out_specs=out_specs,
        out_shape=out_shape,
        compiler_params=_cparams(1),
        name="ffn_mix" if with_mix else ("ffn_proj" if with_proj else "ffn"),
    )(*args)
    return outs if with_proj else outs[0]


def _group_norm_rope(a, gain, cos_t, sin_a, sin_b, scale):
    lane = lax.broadcasted_iota(jnp.int32, (1, LANES), 1)
    first = lane < A_HEAD_DIM
    outs = []
    for p in range(A_W // LANES):
        blk = a[:, p * LANES:(p + 1) * LANES]
        sq = blk * blk
        s0 = jnp.sum(jnp.where(first, sq, 0.0), axis=-1, keepdims=True)
        s1 = jnp.sum(jnp.where(first, 0.0, sq), axis=-1, keepdims=True)
        r = jnp.where(first, lax.rsqrt(s0 * (1.0 / A_HEAD_DIM) + EPS),
                      lax.rsqrt(s1 * (1.0 / A_HEAD_DIM) + EPS))
        y = blk * r * gain[:, p * LANES:(p + 1) * LANES]
        half = ROT_DIM // 2
        y = (y * cos_t + pltpu.roll(y, LANES - half, 1) * sin_a + pltpu.roll(y, half, 1) * sin_b)
        outs.append((y * scale).astype(BF16))
    return jnp.concatenate(outs, axis=-1)


def _mixer_in_proj(x, mod_ref, g_ref, w_ref, gb_ref, gq_ref, gk_ref, cos_ref, sa_ref, sb_ref,
                   mq_ref, mk_ref, mv_ref, mo_ref, gate_ref, aq_ref, ak_ref, av_ref):
    sh = mod_ref[0, 3:4, :]
    sc = mod_ref[0, 4:5, :]
    h = (_rms(x, g_ref[...]) * (1.0 + sc) + sh).astype(BF16)
    proj = jnp.dot(h, w_ref[...], preferred_element_type=F32)

    mq_ref[...] = proj[:, C_MQ:C_MK].astype(BF16)
    mk_ref[...] = (proj[:, C_MK:C_MV] * (M_QK_DIM ** -0.5)).astype(BF16)
    mv_ref[...] = proj[:, C_MV:C_MO].astype(BF16)
    mo_ref[...] = jax.nn.sigmoid(proj[:, C_MO:C_AQ]).astype(BF16)
    av_ref[...] = proj[:, C_AV:C_G].astype(BF16)

    gb = proj[:, C_G:C_G + LANES] + gb_ref[...]
    lane = lax.broadcasted_iota(jnp.int32, (1, LANES), 1)
    is_forget = (lane % (2 * M_HEADS)) >= M_HEADS
    log_sig = jnp.minimum(gb, 0.0) - jnp.log(1.0 + jnp.exp(-jnp.abs(gb)))
    gate_ref[...] = jnp.where(is_forget, log_sig, gb)[:, :N_GATES]

    cos_t, sin_a, sin_b = cos_ref[...], sa_ref[...], sb_ref[...]
    aq_ref[...] = _group_norm_rope(proj[:, C_AQ:C_AK], gq_ref[...], cos_t, sin_a, sin_b,
                                   A_HEAD_DIM ** -0.5 * LOG2E)
    ak_ref[...] = _group_norm_rope(proj[:, C_AK:C_AV], gk_ref[...], cos_t, sin_a, sin_b, 1.0)


def _rope_tables(seq):
    half = ROT_DIM // 2
    inv_freq = ROPE_THETA ** (-2.0 * jnp.arange(half, dtype=F32) / ROT_DIM)
    ang = jnp.arange(seq, dtype=F32)[:, None] * inv_freq[None, :]
    cos, sin = jnp.cos(ang), jnp.sin(ang)
    ones = jnp.ones((seq, A_HEAD_DIM - ROT_DIM), F32)
    zeros_h = jnp.zeros((seq, half), F32)
    zeros_r = jnp.zeros((seq, A_HEAD_DIM - ROT_DIM), F32)
    cos_t = jnp.concatenate([cos, cos, ones], axis=-1)
    sin_a = jnp.concatenate([-sin, zeros_h, zeros_r], axis=-1)
    sin_b = jnp.concatenate([zeros_h, sin, zeros_r], axis=-1)
    tile2 = lambda t: jnp.concatenate([t, t], axis=-1)
    return tile2(cos_t), tile2(sin_a), tile2(sin_b)


def _split3(x):
    hi = x.astype(BF16)
    r1 = x - hi.astype(F32)
    mid = r1.astype(BF16)
    lo = (r1 - mid.astype(F32)).astype(BF16)
    return hi, mid, lo


M_TILE = 16
C_ROWS = M_V_DIM + M_TILE
R_ALPHA, R_BETA, R_WI, R_EMT, R_WG, R_MM = range(6)


def _seg_scan(x, op, fill):
    lane = lax.broadcasted_iota(jnp.int32, (1, LANES), 1)
    first = lane < M_CHUNK
    pos = lane % M_CHUNK
    k = 1
    while k < M_CHUNK:
        from_left = jnp.where(pos >= k, pltpu.roll(x, k, 1), fill)
        from_right = jnp.where(pos < M_CHUNK - k, pltpu.roll(x, LANES - k, 1), fill)
        x = op(x, jnp.where(first, from_left, from_right))
        k *= 2
    return x


def _half_reduce(x, red, fill):
    first = lax.broadcasted_iota(jnp.int32, (1, LANES), 1) < M_CHUNK
    a = red(jnp.where(first, x, fill), axis=-1, keepdims=True)
    b = red(jnp.where(first, fill, x), axis=-1, keepdims=True)
    return jnp.where(first, a, b)


def _mlstm_kernel(q_ref, k_ref, v_ref, og_ref, gi_ref, gf_ref, gmh_ref, out_ref,
                  bt_ref, mg_ref, ms_ref, mn_ref, rows_ref, ut_ref, cst_ref, cs_ref, *, seq):
    lc = M_CHUNK
    nc = seq // lc
    nrow = M_HEADS * nc
    dn_t = (((0,), (0,)), ((), ()))
    dn_nt = (((1,), (1,)), ((), ()))
    lane = lax.broadcasted_iota(jnp.int32, (1, LANES), 1)
    first = lane < lc

    gi = gi_ref[0]
    gf = gf_ref[0]
    b2 = _seg_scan(gf, jnp.add, 0.0)
    u2 = gi - b2
    cmax2 = _seg_scan(u2, jnp.maximum, NEG_INF)
    btot2 = _half_reduce(gf, jnp.sum, 0.0)
    maxu2 = _half_reduce(u2, jnp.max, NEG_INF)
    mg2 = btot2 + maxu2
    bt_ref[...] = btot2
    mg_ref[...] = mg2

    m = jnp.full((M_HEADS, LANES), NEG_INF, F32)
    for i in range(nc):
        rows_f = pl.ds(i, M_HEADS, stride=nc)
        rows_b = pl.ds(nc - 1 - i, M_HEADS, stride=nc)
        bt = jnp.where(first, bt_ref[rows_f, :], bt_ref[rows_b, :])
        mg = jnp.where(first, mg_ref[rows_f, :], mg_ref[rows_b, :])
        m_new = jnp.maximum(bt + m, mg)
        ms_ref[0, rows_f, :] = m
        ms_ref[1, rows_b, :] = m
        mn_ref[0, rows_f, :] = m_new
        mn_ref[1, rows_b, :] = m_new
        m = m_new
    mstart2 = jnp.where(first, ms_ref[0], ms_ref[1])
    mnext2 = jnp.where(first, mn_ref[0], mn_ref[1])

    mm2 = jnp.maximum(mstart2, cmax2)
    rows_ref[R_ALPHA] = jnp.exp(btot2 + mstart2 - mnext2)
    rows_ref[R_BETA] = jnp.exp(mg2 - mnext2)
    rows_ref[R_WI] = jnp.exp(mstart2 - mm2)
    rows_ref[R_EMT] = jnp.exp(-(b2 + mm2))
    rows_ref[R_WG] = jnp.exp(u2 - maxu2)
    rows_ref[R_MM] = mm2

    ut_ref[...] = jnp.zeros(ut_ref.shape, F32)
    for piece, up in enumerate(_split3(u2)):
        up = up.astype(F32)
        by_dir = (up, pltpu.roll(up, lc, 1))
        for h in range(M_HEADS):
            for direction in range(2):
                row = (h % 2) * 6 + direction * 3 + piece
                ut_ref[pl.ds((h // 2) * nc * M_TILE + row, nc, stride=M_TILE), :] = (
                    by_dir[direction][h * nc:(h + 1) * nc])

    cst_ref[...] = jnp.zeros(cst_ref.shape, F32)
    eye = (lax.broadcasted_iota(jnp.int32, (LANES, LANES), 0)
           == lax.broadcasted_iota(jnp.int32, (LANES, LANES), 1))

    def step(i, carry):
        cf = i
        cb = nc - 1 - i
        rf = _aligned(cf * lc, lc)
        rb = _aligned(cb * lc, lc)
        for h in range(M_HEADS):
            vs = slice(h * M_V_DIM, (h + 1) * M_V_DIM)
            st = cst_ref[h]
            st_b = st.astype(BF16)
            l0 = (h % 2) * LANES
            cs_ref[h // 2, cf, :, l0:l0 + lc] = st_b[:, 0:lc]
            cs_ref[h // 2, cb, :, l0 + lc:l0 + LANES] = st_b[:, lc:LANES]
            row_f = h * nc + cf
            row_b = h * nc + cb
            pick = lambda j: jnp.where(first, rows_ref[j, pl.ds(row_f, 1), :],
                                       rows_ref[j, pl.ds(row_b, 1), :])
            alpha, beta, wg = pick(R_ALPHA), pick(R_BETA), pick(R_WG)
            k_f = k_ref[0, pl.ds(rf, lc), vs]
            k_b = k_ref[0, pl.ds(rb, lc), vs]
            zero = jnp.zeros_like(k_f)
            k_bd = jnp.concatenate([jnp.where(first, k_f, zero), jnp.where(first, zero, k_b)], axis=0)
            diag = jnp.where(eye, wg, 0.0).astype(BF16)
            kw = jnp.dot(diag, k_bd, preferred_element_type=F32)
            v_st = jnp.concatenate([v_ref[0, pl.ds(rf, lc), vs], v_ref[0, pl.ds(rb, lc), vs]], axis=0)
            upd = lax.dot_general(v_st, kw.astype(BF16), dn_t, preferred_element_type=F32)
            n_upd = jnp.sum(kw, axis=0, keepdims=True)
            cst_ref[h, 0:M_V_DIM, :] = alpha * st[0:M_V_DIM] + beta * upd
            cst_ref[h, M_V_DIM:M_V_DIM + 1, :] = alpha * st[M_V_DIM:M_V_DIM + 1] + beta * n_upd
        return carry

    lax.fori_loop(0, nc, step, 0, unroll=2)

    pw = 2 * LANES
    lane2 = lax.broadcasted_iota(jnp.int32, (1, pw), 1)
    blk2 = lane2 // lc
    s_i = lax.broadcasted_iota(jnp.int32, (lc, pw), 0)
    t_i = lax.broadcasted_iota(jnp.int32, (lc, pw), 1) % lc
    causal = jnp.where((s_i - t_i) * jnp.where(blk2 % 2 == 0, 1, -1) <= 0, 0.0, NEG_INF).astype(F32)
    tile_row = lax.broadcasted_iota(jnp.int32, (M_TILE, pw), 0)
    spread = jnp.where(tile_row // 3 == lax.broadcasted_iota(jnp.int32, (M_TILE, pw), 1) // lc,
                       1.0, 0.0).astype(BF16)

    def pair_row(j, p, c):
        return jnp.concatenate([rows_ref[j, pl.ds((2 * p) * nc + c, 1), :],
                                rows_ref[j, pl.ds((2 * p + 1) * nc + c, 1), :]], axis=1)

    def chunk(c, carry):
        r0 = _aligned(c * lc, lc)
        rows = pl.ds(r0, lc)
        for p in range(M_HEADS // 2):
            ps = slice(p * pw, (p + 1) * pw)
            k_d = k_ref[0, rows, ps]
            q_d = q_ref[0, rows, ps]
            zero = jnp.zeros_like(q_d)
            q_bd = jnp.concatenate([jnp.where(blk2 == b, q_d, zero) for b in range(4)], axis=0)
            both = lax.dot_general(jnp.concatenate([k_d, cs_ref[p, c]], axis=0), q_bd, dn_nt,
                                   preferred_element_type=F32)
            qk_t = both[0:lc]
            inter = both[lc:]
            tile = pl.ds(_aligned((p * nc + c) * M_TILE, M_TILE), M_TILE)
            u_rows = lax.dot_general(ut_ref[tile, 0:lc].astype(BF16), spread, dn_t,
                                     preferred_element_type=F32)
            dm = u_rows - pair_row(R_MM, p, c) + causal
            s_t = jnp.exp(dm) * qk_t
            den_in = jnp.sum(s_t, axis=0, keepdims=True)
            s_b = s_t.astype(BF16)
            zero_s = jnp.zeros_like(s_b)
            s_bd = jnp.concatenate([jnp.where(lane2 < LANES, s_b, zero_s),
                                    jnp.where(lane2 < LANES, zero_s, s_b)], axis=0)
            v_st = jnp.concatenate([v_ref[0, rows, (2 * p) * M_V_DIM:(2 * p + 1) * M_V_DIM],
                                    v_ref[0, rows, (2 * p + 1) * M_V_DIM:(2 * p + 2) * M_V_DIM]], axis=0)
            num = lax.dot_general(v_st, s_bd, dn_t, preferred_element_type=F32)
            wi = pair_row(R_WI, p, c)
            num = num + wi * inter[0:M_V_DIM]
            den = den_in + wi * inter[M_V_DIM:M_V_DIM + 1]
            h_t = num * (1.0 / jnp.maximum(jnp.abs(den), pair_row(R_EMT, p, c)))
            for hh in range(2):
                vs = slice((2 * p + hh) * M_V_DIM, (2 * p + hh + 1) * M_V_DIM)
                h_n = h_t[:, hh * LANES:(hh + 1) * LANES].T
                hs = h_n[0:lc] + h_n[lc:]
                y = _rms(hs, gmh_ref[:, vs]) * og_ref[0, rows, vs].astype(F32)
                out_ref[0, rows, vs] = y.astype(BF16)
        return carry

    lax.fori_loop(0, nc, chunk, 0, unroll=4)


def _mlstm(mqd, mkd, mv, mo, gi2, gf2, gmh, *, seq):
    bsz = mqd.shape[0]
    nc = seq // M_CHUNK
    nrow = M_HEADS * nc
    assert 2 * M_CHUNK == LANES and 2 * M_QK_DIM == LANES and M_V_DIM == LANES
    b3 = lambda b: (b, 0, 0)
    slab = pltpu.VMEM((nrow, LANES), F32)
    return pl.pallas_call(
        functools.partial(_mlstm_kernel, seq=seq),
        grid=(bsz,),
        in_specs=[pl.BlockSpec((1, seq, M_QKD), b3),
                  pl.BlockSpec((1, seq, M_QKD), b3),
                  pl.BlockSpec((1, seq, M_W), b3),
                  pl.BlockSpec((1, seq, M_W), b3),
                  pl.BlockSpec((1, nrow, LANES), b3),
                  pl.BlockSpec((1, nrow, LANES), b3),
                  pl.BlockSpec((1, M_W), lambda b: (0, 0))],
        out_specs=pl.BlockSpec((1, seq, M_W), b3),
        out_shape=jax.ShapeDtypeStruct((bsz, seq, M_W), BF16),
        scratch_shapes=[slab, slab,
                        pltpu.VMEM((2, nrow, LANES), F32),
                        pltpu.VMEM((2, nrow, LANES), F32),
                        pltpu.VMEM((6, nrow, LANES), F32),
                        pltpu.VMEM((M_HEADS // 2 * nc * M_TILE, LANES), F32),
                        pltpu.VMEM((M_HEADS, C_ROWS, LANES), F32),
                        pltpu.VMEM((M_HEADS // 2, nc, C_ROWS, 2 * LANES), BF16)],
        compiler_params=_cparams(1),
        name="mlstm",
    )(mqd, mkd, mv, mo, gi2, gf2, gmh)


ATT_QB = 128
ATT_SIDE = 64
ATT_KW = ATT_QB + 2 * ATT_SIDE
LOG2E = 1.4426950408889634
assert all(w // (2 * d) == ATT_SIDE for w, d in zip(WINDOWS, DILATIONS))
assert DILATIONS == (1, 4, 16)


def _band_bias(kw, offset):
    qi = lax.broadcasted_iota(jnp.int32, (2 * ATT_QB, kw), 0) % ATT_QB
    ki = lax.broadcasted_iota(jnp.int32, (2 * ATT_QB, kw), 1)
    return jnp.where(jnp.abs(ki + offset - qi) <= ATT_SIDE, 0.0, NEG_INF).astype(F32)


def _attn_block(qb, kb, vb, bias):
    first = lax.broadcasted_iota(jnp.int32, (1, LANES), 1) < A_HEAD_DIM
    zero = jnp.zeros_like(qb)
    qm = jnp.concatenate([jnp.where(first, qb, zero), jnp.where(first, zero, qb)], axis=0)
    s = lax.dot_general(qm, kb, (((1,), (1,)), ((), ())), preferred_element_type=F32) + bias
    m = jnp.max(s, axis=-1, keepdims=True)
    p = jnp.exp2(s - m)
    den = jnp.sum(p, axis=-1, keepdims=True)
    o = jnp.dot(p.astype(BF16), vb, preferred_element_type=F32) / den
    lse = m + jnp.log2(den)
    return (jnp.where(first, o[:ATT_QB], o[ATT_QB:]),
            jnp.where(first, lse[:ATT_QB], lse[ATT_QB:]))


def _merge(o_a, l_a, o_b, l_b):
    top = jnp.maximum(l_a, l_b)
    w_a = jnp.exp2(l_a - top)
    w_b = jnp.exp2(l_b - top)
    tot = w_a + w_b
    return (w_a * o_a + w_b * o_b) / tot, top + jnp.log2(tot)


def _dil_attn_kernel(q_ref, k_ref, v_ref, out_ref, tok_ref, r4f_ref, r4_ref, r16_ref,
                     o4_ref, l4_ref, ot_ref, lt_ref, bias_ref, *, seq):
    l4 = seq // 4
    l16 = seq // 16
    n_in = 3

    @pl.when((pl.program_id(0) == 0) & (pl.program_id(1) == 0))
    def _():
        bias_ref[0] = _band_bias(ATT_KW, 0)
        bias_ref[1] = _band_bias(ATT_KW, -ATT_SIDE)
        bias_ref[2] = _band_bias(ATT_KW, ATT_QB - ATT_KW)
        bias_ref[3, :, :ATT_QB] = _band_bias(ATT_QB, 0)

    for i, ref in enumerate((q_ref, k_ref, v_ref)):
        tok_ref[i] = ref[0].astype(F32)
    for i in range(n_in):
        for r in range(4):
            x = tok_ref[i, pl.ds(r, l4, stride=4), :]
            r4f_ref[i, r * l4:(r + 1) * l4, :] = x
            r4_ref[i, r * l4:(r + 1) * l4, :] = x.astype(BF16)
    for i in range(n_in):
        for c in range(16):
            x = r4f_ref[i, pl.ds((c // 4) * l4 + c % 4, l16, stride=4), :]
            r16_ref[i, c * l16:(c + 1) * l16, :] = x.astype(BF16)

    bias3 = bias_ref[3, :, :ATT_QB]

    def pat3(r4, carry):
        for r2 in range(4):
            q0 = _aligned((r4 * 4 + r2) * l16, l16)
            rows = pl.ds(q0, l16)
            o, l = _attn_block(r16_ref[0, rows, :], r16_ref[1, rows, :], r16_ref[2, rows, :], bias3)
            dst = pl.ds(r4 * l4 + r2, l16, stride=4)
            o4_ref[dst, :] = o
            l4_ref[dst, :] = l
        return carry

    lax.fori_loop(0, 4, pat3, 0, unroll=2)

    def window(ref, i, q0, jb, nb):
        off = 0 if jb == 0 else (ATT_QB - ATT_KW if jb == nb - 1 else -ATT_SIDE)
        return ref[i, pl.ds(_aligned(q0 + off, ATT_SIDE), ATT_KW), :]

    def bias_of(jb, nb):
        return bias_ref[0 if jb == 0 else (2 if jb == nb - 1 else 1)]

    nb4 = l4 // ATT_QB

    def pat2(r, carry):
        for jb in range(nb4):
            q0 = _aligned(r * l4 + jb * ATT_QB, ATT_QB)
            rows = pl.ds(q0, ATT_QB)
            o, l = _attn_block(r4_ref[0, rows, :], window(r4_ref, 1, q0, jb, nb4),
                               window(r4_ref, 2, q0, jb, nb4), bias_of(jb, nb4))
            o, l = _merge(o, l, o4_ref[rows, :], l4_ref[rows, :])
            dst = pl.ds(4 * jb * ATT_QB + r, ATT_QB, stride=4)
            ot_ref[dst, :] = o
            lt_ref[dst, :] = l
        return carry

    lax.fori_loop(0, 4, pat2, 0, unroll=2)

    nb1 = seq // ATT_QB

    def pat1_block(n, jb):
        q0 = _aligned(n * ATT_QB, ATT_QB)
        rows = pl.ds(q0, ATT_QB)
        o, l = _attn_block(q_ref[0, rows, :], window(k_ref, 0, q0, jb, nb1),
                           window(v_ref, 0, q0, jb, nb1), bias_of(jb, nb1))
        o, _ = _merge(o, l, ot_ref[rows, :], lt_ref[rows, :])
        out_ref[0, rows, :] = o.astype(BF16)

    per_iter = (nb1 - 2) // 2

    def pat1(g, carry):
        for u in range(per_iter):
            pat1_block(per_iter * g + 1 + u, 1)
        return carry

    pat1_block(0, 0)
    lax.fori_loop(0, 2, pat1, 0)
    for n in range(2 * per_iter + 1, nb1):
        pat1_block(n, n)


def _dil_attn(aq, ak, av, *, seq):
    bsz = aq.shape[0]
    n_pairs = A_W // LANES
    blk = pl.BlockSpec((1, seq, LANES), lambda b, p: (b, 0, p))
    f32_slab = pltpu.VMEM((seq, LANES), F32)
    return pl.pallas_call(
        functools.partial(_dil_attn_kernel, seq=seq),
        grid=(bsz, n_pairs),
        in_specs=[blk, blk, blk],
        out_specs=blk,
        out_shape=jax.ShapeDtypeStruct((bsz, seq, A_W), BF16),
        scratch_shapes=[pltpu.VMEM((3, seq, LANES), F32),
                        pltpu.VMEM((3, seq, LANES), F32),
                        pltpu.VMEM((3, seq, LANES), BF16),
                        pltpu.VMEM((3, seq, LANES), BF16),
                        f32_slab, f32_slab, f32_slab, f32_slab,
                        pltpu.VMEM((4, 2 * ATT_QB, ATT_KW), F32)],
        compiler_params=_cparams(2),
        name="dil_attn",
    )(aq, ak, av)


def _layer(x, mod3, g_ffn1, w_gu1, w_down1, g_mix, w_in, gate_bias, g_q, g_k, g_mh, w_out,
           g_ffn2, w_gu2, w_down2, g_final):
    bsz, seq, d = x.shape
    t = bsz * seq
    tm = 512
    row2 = lambda v: v.reshape(1, -1).astype(F32)

    o_mk, o_mv = M_QKW, 2 * M_QKW
    o_g = o_mv + 2 * M_W
    o_aq = o_g + N_GATES

    def twice_per_head(w):
        wh = w.reshape(d, M_HEADS, 1, M_QK_DIM)
        return jnp.broadcast_to(wh, (d, M_HEADS, 2, M_QK_DIM)).reshape(d, M_QKD)

    w_cat = jnp.concatenate(
        [twice_per_head(w_in[:, :o_mk]), twice_per_head(w_in[:, o_mk:o_mv]), w_in[:, o_mv:o_g],
         w_in[:, o_aq:], w_in[:, o_g:o_aq],
         jnp.zeros((d, LANES - N_GATES), w_in.dtype)], axis=1).astype(BF16)
    gb = jnp.concatenate([gate_bias.reshape(1, N_GATES).astype(F32),
                          jnp.zeros((1, LANES - N_GATES), F32)], axis=1)
    gq = jnp.tile(row2(g_q), (1, A_HEADS))
    gk = jnp.tile(row2(g_k), (1, A_HEADS))
    x1, mq, mk, mv, mo, gates, aq, ak, av = _ffn(
        x.reshape(t, d), mod3, row2(g_ffn1), w_gu1.astype(BF16), w_down1.astype(BF16),
        seq=seq, mod_base=0, tm=tm,
        proj=(row2(g_mix), w_cat, gb, gq, gk, _rope_tables(seq)))

    to3 = lambda a: a.reshape(bsz, seq, a.shape[-1])
    nc = seq // M_CHUNK
    g5 = gates.reshape(bsz, nc, M_CHUNK, 2, 2, M_HEADS)
    g5 = g5.transpose(4, 0, 5, 1, 3, 2).reshape(2, bsz, M_HEADS * nc, 2 * M_CHUNK)
    hm = _mlstm(to3(mq), to3(mk), to3(mv), to3(mo), g5[0], g5[1], row2(g_mh), seq=seq)
    ha = _dil_attn(to3(aq), to3(ak), to3(av), seq=seq)

    w_out_b = w_out.astype(BF16)
    out = _ffn(x1, mod3, row2(g_ffn2), w_gu2.astype(BF16), w_down2.astype(BF16),
               seq=seq, mod_base=6, tm=tm,
               mix=(hm.reshape(t, M_W), ha.reshape(t, A_W), w_out_b[:M_W], w_out_b[M_W:]),
               gfin=row2(g_final))
    return out.reshape(bsz, seq, d)


def kernel(x, c, w_ada, b_ada, g_ffn1, w_gu1, w_down1, g_mix, w_in, gate_bias, g_q, g_k, g_mh,
           w_out, g_ffn2, w_gu2, w_down2, g_final):
    bsz, seq, d = x.shape
    depth = w_ada.shape[0]
    for l in range(depth):
        mod = _adaln(c, w_ada[l], b_ada[l].reshape(1, -1))
        mod3 = mod.reshape(bsz, N_MOD, d)
        x = _layer(x, mod3, g_ffn1[l], w_gu1[l], w_down1[l], g_mix[l], w_in[l], gate_bias[l],
                   g_q[l], g_k[l], g_mh[l], w_out[l], g_ffn2[l], w_gu2[l], w_down2[l], g_final[l])
    return x
```

```python
import functools

import jax
import jax.numpy as jnp
from jax import lax
from jax.experimental import pallas as pl
from jax.experimental.pallas import tpu as pltpu

F32 = jnp.float32
BF16 = jnp.bfloat16

D_MODEL = 1024
D_FF = 2816
N_MOD = 9
M_HEADS = 4
M_V_DIM = 128
M_QK_DIM = 64
M_CHUNK = 64
A_HEADS = 8
A_HEAD_DIM = 64
WINDOWS = (128, 512, 2048)
DILATIONS = (1, 4, 16)
ROT_DIM = A_HEAD_DIM // 4
ROPE_THETA = 500000.0
EPS = 1e-6
NEG_INF = -1e30
HALF_STEP = 0.5

LANES = 128
M_W = M_HEADS * M_V_DIM
M_QKW = M_HEADS * M_QK_DIM
A_W = A_HEADS * A_HEAD_DIM
N_GATES = 4 * M_HEADS
M_QKD = 2 * M_QKW
C_MQ, C_MK, C_MV, C_MO = 0, M_QKD, 2 * M_QKD, 2 * M_QKD + M_W
C_AQ = C_MO + M_W
C_AK = C_AQ + A_W
C_AV = C_AK + A_W
C_G = C_AV + A_W
IN_COLS_PAD = C_G + LANES

VMEM_LIMIT = 56 * 1024 * 1024


def _cparams(n_axes):
    return pltpu.CompilerParams(dimension_semantics=("arbitrary",) * n_axes,
                                vmem_limit_bytes=VMEM_LIMIT)


def _aligned(x, m):
    return x if isinstance(x, int) else pl.multiple_of(x, m)


def _rms(xf, g):
    ms = jnp.mean(xf * xf, axis=-1, keepdims=True)
    return xf * lax.rsqrt(ms + EPS) * g


def _adaln_kernel(c_ref, w_ref, b_ref, o_ref):
    c = c_ref[...]
    cs = (c * jax.nn.sigmoid(c)).astype(BF16)
    o_ref[...] = jnp.dot(cs, w_ref[...].astype(BF16), preferred_element_type=F32) + b_ref[...]


def _adaln(c, w, b):
    bsz, d = c.shape
    n = w.shape[1]
    tn = 1024
    return pl.pallas_call(
        _adaln_kernel,
        grid=(n // tn,),
        in_specs=[pl.BlockSpec((bsz, d), lambda j: (0, 0)),
                  pl.BlockSpec((d, tn), lambda j: (0, j)),
                  pl.BlockSpec((1, tn), lambda j: (0, j))],
        out_specs=pl.BlockSpec((bsz, tn), lambda j: (0, j)),
        out_shape=jax.ShapeDtypeStruct((bsz, n), F32),
        compiler_params=_cparams(1),
        name="adaln",
    )(c, w, b)


N_PROJ_IN = 8
N_PROJ_OUT = 8


def _ffn_kernel(*refs, mod_base, with_mix, final_norm, with_proj):
    it = iter(refs)
    x_ref = next(it)
    if with_mix:
        hm_ref, ha_ref, wo_m_ref, wo_a_ref = next(it), next(it), next(it), next(it)
    mod_ref, g_ref, wgu_ref, wd_ref = next(it), next(it), next(it), next(it)
    if final_norm:
        gfin_ref = next(it)
    if with_proj:
        proj_in = [next(it) for _ in range(N_PROJ_IN)]
    o_ref = next(it)
    if with_proj:
        proj_out = [next(it) for _ in range(N_PROJ_OUT)]

    x = x_ref[...]
    if with_mix:
        gt_mix = mod_ref[0, mod_base - 1:mod_base, :]
        y = (jnp.dot(hm_ref[...], wo_m_ref[...], preferred_element_type=F32)
             + jnp.dot(ha_ref[...], wo_a_ref[...], preferred_element_type=F32))
        x = x + gt_mix * y
    sh = mod_ref[0, mod_base:mod_base + 1, :]
    sc = mod_ref[0, mod_base + 1:mod_base + 2, :]
    gt = mod_ref[0, mod_base + 2:mod_base + 3, :]
    h = (_rms(x, g_ref[...]) * (1.0 + sc) + sh).astype(BF16)
    gu = jnp.dot(h, wgu_ref[...], preferred_element_type=F32)
    g = gu[:, :D_FF]
    u = gu[:, D_FF:]
    a = (g * jax.nn.sigmoid(g) * u).astype(BF16)
    y = jnp.dot(a, wd_ref[...], preferred_element_type=F32)
    x = x + HALF_STEP * gt * y
    if final_norm:
        x = _rms(x, gfin_ref[...])
    o_ref[...] = x
    if with_proj:
        _mixer_in_proj(x, mod_ref, *proj_in, *proj_out)


def _ffn(x2d, mod3, g, wgu, wd, *, seq, mod_base, tm, mix=None, gfin=None, proj=None):
    t, d = x2d.shape
    with_mix = mix is not None
    final_norm = gfin is not None
    with_proj = proj is not None
    row = lambda i: (i, 0)
    const = lambda i: (0, 0)
    resident = pl.Buffered(1)
    args = [x2d]
    in_specs = [pl.BlockSpec((tm, d), row)]
    if with_mix:
        hm, ha, wo_m, wo_a = mix
        args += [hm, ha, wo_m, wo_a]
        in_specs += [pl.BlockSpec((tm, M_W), row), pl.BlockSpec((tm, A_W), row),
                     pl.BlockSpec((M_W, d), const, pipeline_mode=resident),
                     pl.BlockSpec((A_W, d), const, pipeline_mode=resident)]
    args += [mod3, g, wgu, wd]
    in_specs += [pl.BlockSpec((1, N_MOD, d), lambda i: ((i * tm) // seq, 0, 0)),
                 pl.BlockSpec((1, d), const),
                 pl.BlockSpec((d, 2 * D_FF), const, pipeline_mode=resident),
                 pl.BlockSpec((D_FF, d), const, pipeline_mode=resident)]
    if final_norm:
        args.append(gfin)
        in_specs.append(pl.BlockSpec((1, d), const))
    out_shape = [jax.ShapeDtypeStruct((t, d), F32)]
    if with_proj:
        g_mix, w_cat, gate_bias, gq, gk, tables = proj
        pos = lambda i: ((i * tm) % seq // tm, 0)
        args += [g_mix, w_cat, gate_bias, gq, gk, *tables]
        in_specs += [pl.BlockSpec((1, d), const),
                     pl.BlockSpec((d, IN_COLS_PAD), const, pipeline_mode=resident),
                     pl.BlockSpec((1, LANES), const),
                     pl.BlockSpec((1, A_W), const),
                     pl.BlockSpec((1, A_W), const),
                     pl.BlockSpec((tm, LANES), pos),
                     pl.BlockSpec((tm, LANES), pos),
                     pl.BlockSpec((tm, LANES), pos)]
        bf = lambda n: jax.ShapeDtypeStruct((t, n), BF16)
        out_shape += [bf(M_QKD), bf(M_QKD), bf(M_W), bf(M_W),
                      jax.ShapeDtypeStruct((t, N_GATES), F32), bf(A_W), bf(A_W), bf(A_W)]
    out_specs = [pl.BlockSpec((tm, s.shape[1]), row) for s in out_shape]
    outs = pl.pallas_call(
        functools.partial(_ffn_kernel, mod_base=mod_base, with_mix=with_mix, final_norm=final_norm,
                          with_proj=with_proj),
        grid=(t // tm,),
        in_specs=in_specs,
        out_specs=out_specs,
        out_shape=out_shape,
        compiler_params=_cparams(1),
        name="ffn_mix" if with_mix else ("ffn_proj" if with_proj else "ffn"),
    )(*args)
    return outs if with_proj else outs[0]


def _group_norm_rope(a, gain, cos_t, sin_a, sin_b, scale):
    lane = lax.broadcasted_iota(jnp.int32, (1, LANES), 1)
    first = lane < A_HEAD_DIM
    outs = []
    for p in range(A_W // LANES):
        blk = a[:, p * LANES:(p + 1) * LANES]
        sq = blk * blk
        s0 = jnp.sum(jnp.where(first, sq, 0.0), axis=-1, keepdims=True)
        s1 = jnp.sum(jnp.where(first, 0.0, sq), axis=-1, keepdims=True)
        r = jnp.where(first, lax.rsqrt(s0 * (1.0 / A_HEAD_DIM) + EPS),
                      lax.rsqrt(s1 * (1.0 / A_HEAD_DIM) + EPS))
        y = blk * r * gain[:, p * LANES:(p + 1) * LANES]
        half = ROT_DIM // 2
        y = (y * cos_t + pltpu.roll(y, LANES - half, 1) * sin_a + pltpu.roll(y, half, 1) * sin_b)
        outs.append((y * scale).astype(BF16))
    return jnp.concatenate(outs, axis=-1)


def _mixer_in_proj(x, mod_ref, g_ref, w_ref, gb_ref, gq_ref, gk_ref, cos_ref, sa_ref, sb_ref,
                   mq_ref, mk_ref, mv_ref, mo_ref, gate_ref, aq_ref, ak_ref, av_ref):
    sh = mod_ref[0, 3:4, :]
    sc = mod_ref[0, 4:5, :]
    h = (_rms(x, g_ref[...]) * (1.0 + sc) + sh).astype(BF16)
    proj = jnp.dot(h, w_ref[...], preferred_element_type=F32)

    mq_ref[...] = proj[:, C_MQ:C_MK].astype(BF16)
    mk_ref[...] = (proj[:, C_MK:C_MV] * (M_QK_DIM ** -0.5)).astype(BF16)
    mv_ref[...] = proj[:, C_MV:C_MO].astype(BF16)
    mo_ref[...] = jax.nn.sigmoid(proj[:, C_MO:C_AQ]).astype(BF16)
    av_ref[...] = proj[:, C_AV:C_G].astype(BF16)

    gb = proj[:, C_G:C_G + LANES] + gb_ref[...]
    lane = lax.broadcasted_iota(jnp.int32, (1, LANES), 1)
    is_forget = (lane % (2 * M_HEADS)) >= M_HEADS
    log_sig = jnp.minimum(gb, 0.0) - jnp.log(1.0 + jnp.exp(-jnp.abs(gb)))
    gate_ref[...] = jnp.where(is_forget, log_sig, gb)[:, :N_GATES]

    cos_t, sin_a, sin_b = cos_ref[...], sa_ref[...], sb_ref[...]
    aq_ref[...] = _group_norm_rope(proj[:, C_AQ:C_AK], gq_ref[...], cos_t, sin_a, sin_b,
                                   A_HEAD_DIM ** -0.5 * LOG2E)
    ak_ref[...] = _group_norm_rope(proj[:, C_AK:C_AV], gk_ref[...], cos_t, sin_a, sin_b, 1.0)


def _rope_tables(seq):
    half = ROT_DIM // 2
    inv_freq = ROPE_THETA ** (-2.0 * jnp.arange(half, dtype=F32) / ROT_DIM)
    ang = jnp.arange(seq, dtype=F32)[:, None] * inv_freq[None, :]
    cos, sin = jnp.cos(ang), jnp.sin(ang)
    ones = jnp.ones((seq, A_HEAD_DIM - ROT_DIM), F32)
    zeros_h = jnp.zeros((seq, half), F32)
    zeros_r = jnp.zeros((seq, A_HEAD_DIM - ROT_DIM), F32)
    cos_t = jnp.concatenate([cos, cos, ones], axis=-1)
    sin_a = jnp.concatenate([-sin, zeros_h, zeros_r], axis=-1)
    sin_b = jnp.concatenate([zeros_h, sin, zeros_r], axis=-1)
    tile2 = lambda t: jnp.concatenate([t, t], axis=-1)
    return tile2(cos_t), tile2(sin_a), tile2(sin_b)


def _split3(x):
    hi = x.astype(BF16)
    r1 = x - hi.astype(F32)
    mid = r1.astype(BF16)
    lo = (r1 - mid.astype(F32)).astype(BF16)
    return hi, mid, lo


M_TILE = 16
C_ROWS = M_V_DIM + M_TILE
R_ALPHA, R_BETA, R_WI, R_EMT, R_WG, R_MM = range(6)


def _seg_scan(x, op, fill):
    lane = lax.broadcasted_iota(jnp.int32, (1, LANES), 1)
    first = lane < M_CHUNK
    pos = lane % M_CHUNK
    k = 1
    while k < M_CHUNK:
        from_left = jnp.where(pos >= k, pltpu.roll(x, k, 1), fill)
        from_right = jnp.where(pos < M_CHUNK - k, pltpu.roll(x, LANES - k, 1), fill)
        x = op(x, jnp.where(first, from_left, from_right))
        k *= 2
    return x


def _half_reduce(x, red, fill):
    first = lax.broadcasted_iota(jnp.int32, (1, LANES), 1) < M_CHUNK
    a = red(jnp.where(first, x, fill), axis=-1, keepdims=True)
    b = red(jnp.where(first, fill, x), axis=-1, keepdims=True)
    return jnp.where(first, a, b)


def _mlstm_kernel(q_ref, k_ref, v_ref, og_ref, gi_ref, gf_ref, gmh_ref, out_ref,
                  bt_ref, mg_ref, ms_ref, mn_ref, rows_ref, ut_ref, cst_ref, cs_ref, *, seq):
    lc = M_CHUNK
    nc = seq // lc
    nrow = M_HEADS * nc
    dn_t = (((0,), (0,)), ((), ()))
    dn_nt = (((1,), (1,)), ((), ()))
    lane = lax.broadcasted_iota(jnp.int32, (1, LANES), 1)
    first = lane < lc

    gi = gi_ref[0]
    gf = gf_ref[0]
    b2 = _seg_scan(gf, jnp.add, 0.0)
    u2 = gi - b2
    cmax2 = _seg_scan(u2, jnp.maximum, NEG_INF)
    btot2 = _half_reduce(gf, jnp.sum, 0.0)
    maxu2 = _half_reduce(u2, jnp.max, NEG_INF)
    mg2 = btot2 + maxu2
    bt_ref[...] = btot2
    mg_ref[...] = mg2

    m = jnp.full((M_HEADS, LANES), NEG_INF, F32)
    for i in range(nc):
        rows_f = pl.ds(i, M_HEADS, stride=nc)
        rows_b = pl.ds(nc - 1 - i, M_HEADS, stride=nc)
        bt = jnp.where(first, bt_ref[rows_f, :], bt_ref[rows_b, :])
        mg = jnp.where(first, mg_ref[rows_f, :], mg_ref[rows_b, :])
        m_new = jnp.maximum(bt + m, mg)
        ms_ref[0, rows_f, :] = m
        ms_ref[1, rows_b, :] = m
        mn_ref[0, rows_f, :] = m_new
        mn_ref[1, rows_b, :] = m_new
        m = m_new
    mstart2 = jnp.where(first, ms_ref[0], ms_ref[1])
    mnext2 = jnp.where(first, mn_ref[0], mn_ref[1])

    mm2 = jnp.maximum(mstart2, cmax2)
    rows_ref[R_ALPHA] = jnp.exp(btot2 + mstart2 - mnext2)
    rows_ref[R_BETA] = jnp.exp(mg2 - mnext2)
    rows_ref[R_WI] = jnp.exp(mstart2 - mm2)
    rows_ref[R_EMT] = jnp.exp(-(b2 + mm2))
    rows_ref[R_WG] = jnp.exp(u2 - maxu2)
    rows_ref[R_MM] = mm2

    ut_ref[...] = jnp.zeros(ut_ref.shape, F32)
    for piece, up in enumerate(_split3(u2)):
        up = up.astype(F32)
        by_dir = (up, pltpu.roll(up, lc, 1))
        for h in range(M_HEADS):
            for direction in range(2):
                row = (h % 2) * 6 + direction * 3 + piece
                ut_ref[pl.ds((h // 2) * nc * M_TILE + row, nc, stride=M_TILE), :] = (
                    by_dir[direction][h * nc:(h + 1) * nc])

    cst_ref[...] = jnp.zeros(cst_ref.shape, F32)
    eye = (lax.broadcasted_iota(jnp.int32, (LANES, LANES), 0)
           == lax.broadcasted_iota(jnp.int32, (LANES, LANES), 1))

    def step(i, carry):
        cf = i
        cb = nc - 1 - i
        rf = _aligned(cf * lc, lc)
        rb = _aligned(cb * lc, lc)
        for h in range(M_HEADS):
            vs = slice(h * M_V_DIM, (h + 1) * M_V_DIM)
            st = cst_ref[h]
            st_b = st.astype(BF16)
            l0 = (h % 2) * LANES
            cs_ref[h // 2, cf, :, l0:l0 + lc] = st_b[:, 0:lc]
            cs_ref[h // 2, cb, :, l0 + lc:l0 + LANES] = st_b[:, lc:LANES]
            row_f = h * nc + cf
            row_b = h * nc + cb
            pick = lambda j: jnp.where(first, rows_ref[j, pl.ds(row_f, 1), :],
                                       rows_ref[j, pl.ds(row_b, 1), :])
            alpha, beta, wg = pick(R_ALPHA), pick(R_BETA), pick(R_WG)
            k_f = k_ref[0, pl.ds(rf, lc), vs]
            k_b = k_ref[0, pl.ds(rb, lc), vs]
            zero = jnp.zeros_like(k_f)
            k_bd = jnp.concatenate([jnp.where(first, k_f, zero), jnp.where(first, zero, k_b)], axis=0)
            diag = jnp.where(eye, wg, 0.0).astype(BF16)
            kw = jnp.dot(diag, k_bd, preferred_element_type=F32)
            v_st = jnp.concatenate([v_ref[0, pl.ds(rf, lc), vs], v_ref[0, pl.ds(rb, lc), vs]], axis=0)
            upd = lax.dot_general(v_st, kw.astype(BF16), dn_t, preferred_element_type=F32)
            n_upd = jnp.sum(kw, axis=0, keepdims=True)
            cst_ref[h, 0:M_V_DIM, :] = alpha * st[0:M_V_DIM] + beta * upd
            cst_ref[h, M_V_DIM:M_V_DIM + 1, :] = alpha * st[M_V_DIM:M_V_DIM + 1] + beta * n_upd
        return carry

    lax.fori_loop(0, nc, step, 0, unroll=8)

    pw = 2 * LANES
    lane2 = lax.broadcasted_iota(jnp.int32, (1, pw), 1)
    blk2 = lane2 // lc
    s_i = lax.broadcasted_iota(jnp.int32, (lc, pw), 0)
    t_i = lax.broadcasted_iota(jnp.int32, (lc, pw), 1) % lc
    causal = jnp.where((s_i - t_i) * jnp.where(blk2 % 2 == 0, 1, -1) <= 0, 0.0, NEG_INF).astype(F32)
    tile_row = lax.broadcasted_iota(jnp.int32, (M_TILE, pw), 0)
    spread = jnp.where(tile_row // 3 == lax.broadcasted_iota(jnp.int32, (M_TILE, pw), 1) // lc,
                       1.0, 0.0).astype(BF16)

    def pair_row(j, p, c):
        return jnp.concatenate([rows_ref[j, pl.ds((2 * p) * nc + c, 1), :],
                                rows_ref[j, pl.ds((2 * p + 1) * nc + c, 1), :]], axis=1)

    def chunk(c, carry):
        r0 = _aligned(c * lc, lc)
        rows = pl.ds(r0, lc)
        for p in range(M_HEADS // 2):
            ps = slice(p * pw, (p + 1) * pw)
            k_d = k_ref[0, rows, ps]
            q_d = q_ref[0, rows, ps]
            zero = jnp.zeros_like(q_d)
            q_bd = jnp.concatenate([jnp.where(blk2 == b, q_d, zero) for b in range(4)], axis=0)
            both = lax.dot_general(jnp.concatenate([k_d, cs_ref[p, c]], axis=0), q_bd, dn_nt,
                                   preferred_element_type=F32)
            qk_t = both[0:lc]
            inter = both[lc:]
            tile = pl.ds(_aligned((p * nc + c) * M_TILE, M_TILE), M_TILE)
            u_rows = lax.dot_general(ut_ref[tile, 0:lc].astype(BF16), spread, dn_t,
                                     preferred_element_type=F32)
            dm = u_rows - pair_row(R_MM, p, c) + causal
            s_t = jnp.exp(dm) * qk_t
            den_in = jnp.sum(s_t, axis=0, keepdims=True)
            s_b = s_t.astype(BF16)
            zero_s = jnp.zeros_like(s_b)
            s_bd = jnp.concatenate([jnp.where(lane2 < LANES, s_b, zero_s),
                                    jnp.where(lane2 < LANES, zero_s, s_b)], axis=0)
            v_st = jnp.concatenate([v_ref[0, rows, (2 * p) * M_V_DIM:(2 * p + 1) * M_V_DIM],
                                    v_ref[0, rows, (2 * p + 1) * M_V_DIM:(2 * p + 2) * M_V_DIM]], axis=0)
            num = lax.dot_general(v_st, s_bd, dn_t, preferred_element_type=F32)
            wi = pair_row(R_WI, p, c)
            num = num + wi * inter[0:M_V_DIM]
            den = den_in + wi * inter[M_V_DIM:M_V_DIM + 1]
            h_t = num * (1.0 / jnp.maximum(jnp.abs(den), pair_row(R_EMT, p, c)))
            for hh in range(2):
                vs = slice((2 * p + hh) * M_V_DIM, (2 * p + hh + 1) * M_V_DIM)
                h_n = h_t[:, hh * LANES:(hh + 1) * LANES].T
                hs = h_n[0:lc] + h_n[lc:]
                y = _rms(hs, gmh_ref[:, vs]) * og_ref[0, rows, vs].astype(F32)
                out_ref[0, rows, vs] = y.astype(BF16)
        return carry

    lax.fori_loop(0, nc, chunk, 0, unroll=4)


def _mlstm(mqd, mkd, mv, mo, gi2, gf2, gmh, *, seq):
    bsz = mqd.shape[0]
    nc = seq // M_CHUNK
    nrow = M_HEADS * nc
    assert 2 * M_CHUNK == LANES and 2 * M_QK_DIM == LANES and M_V_DIM == LANES
    b3 = lambda b: (b, 0, 0)
    slab = pltpu.VMEM((nrow, LANES), F32)
    return pl.pallas_call(
        functools.partial(_mlstm_kernel, seq=seq),
        grid=(bsz,),
        in_specs=[pl.BlockSpec((1, seq, M_QKD), b3),
                  pl.BlockSpec((1, seq, M_QKD), b3),
                  pl.BlockSpec((1, seq, M_W), b3),
                  pl.BlockSpec((1, seq, M_W), b3),
                  pl.BlockSpec((1, nrow, LANES), b3),
                  pl.BlockSpec((1, nrow, LANES), b3),
                  pl.BlockSpec((1, M_W), lambda b: (0, 0))],
        out_specs=pl.BlockSpec((1, seq, M_W), b3),
        out_shape=jax.ShapeDtypeStruct((bsz, seq, M_W), BF16),
        scratch_shapes=[slab, slab,
                        pltpu.VMEM((2, nrow, LANES), F32),
                        pltpu.VMEM((2, nrow, LANES), F32),
                        pltpu.VMEM((6, nrow, LANES), F32),
                        pltpu.VMEM((M_HEADS // 2 * nc * M_TILE, LANES), F32),
                        pltpu.VMEM((M_HEADS, C_ROWS, LANES), F32),
                        pltpu.VMEM((M_HEADS // 2, nc, C_ROWS, 2 * LANES), BF16)],
        compiler_params=_cparams(1),
        name="mlstm",
    )(mqd, mkd, mv, mo, gi2, gf2, gmh)


ATT_QB = 128
ATT_SIDE = 64
ATT_KW = ATT_QB + 2 * ATT_SIDE
LOG2E = 1.4426950408889634
assert all(w // (2 * d) == ATT_SIDE for w, d in zip(WINDOWS, DILATIONS))
assert DILATIONS == (1, 4, 16)


def _band_bias(kw, offset):
    qi = lax.broadcasted_iota(jnp.int32, (2 * ATT_QB, kw), 0) % ATT_QB
    ki = lax.broadcasted_iota(jnp.int32, (2 * ATT_QB, kw), 1)
    return jnp.where(jnp.abs(ki + offset - qi) <= ATT_SIDE, 0.0, NEG_INF).astype(F32)


def _attn_block(qb, kb, vb, bias):
    first = lax.broadcasted_iota(jnp.int32, (1, LANES), 1) < A_HEAD_DIM
    zero = jnp.zeros_like(qb)
    qm = jnp.concatenate([jnp.where(first, qb, zero), jnp.where(first, zero, qb)], axis=0)
    s = lax.dot_general(qm, kb, (((1,), (1,)), ((), ())), preferred_element_type=F32) + bias
    m = jnp.max(s, axis=-1, keepdims=True)
    p = jnp.exp2(s - m)
    den = jnp.sum(p, axis=-1, keepdims=True)
    o = jnp.dot(p.astype(BF16), vb, preferred_element_type=F32) / den
    lse = m + jnp.log2(den)
    return (jnp.where(first, o[:ATT_QB], o[ATT_QB:]),
            jnp.where(first, lse[:ATT_QB], lse[ATT_QB:]))


def _merge(o_a, l_a, o_b, l_b):
    top = jnp.maximum(l_a, l_b)
    w_a = jnp.exp2(l_a - top)
    w_b = jnp.exp2(l_b - top)
    tot = w_a + w_b
    return (w_a * o_a + w_b * o_b) / tot, top + jnp.log2(tot)


def _dil_attn_kernel(q_ref, k_ref, v_ref, out_ref, tok_ref, r4f_ref, r4_ref, r16_ref,
                     o4_ref, l4_ref, ot_ref, lt_ref, bias_ref, *, seq):
    l4 = seq // 4
    l16 = seq // 16
    n_in = 3

    @pl.when((pl.program_id(0) == 0) & (pl.program_id(1) == 0))
    def _():
        bias_ref[0] = _band_bias(ATT_KW, 0)
        bias_ref[1] = _band_bias(ATT_KW, -ATT_SIDE)
        bias_ref[2] = _band_bias(ATT_KW, ATT_QB - ATT_KW)
        bias_ref[3, :, :ATT_QB] = _band_bias(ATT_QB, 0)

    for i, ref in enumerate((q_ref, k_ref, v_ref)):
        tok_ref[i] = ref[0].astype(F32)
    for i in range(n_in):
        for r in range(4):
            x = tok_ref[i, pl.ds(r, l4, stride=4), :]
            r4f_ref[i, r * l4:(r + 1) * l4, :] = x
            r4_ref[i, r * l4:(r + 1) * l4, :] = x.astype(BF16)
    for i in range(n_in):
        for c in range(16):
            x = r4f_ref[i, pl.ds((c // 4) * l4 + c % 4, l16, stride=4), :]
            r16_ref[i, c * l16:(c + 1) * l16, :] = x.astype(BF16)

    bias3 = bias_ref[3, :, :ATT_QB]

    def pat3(r4, carry):
        for r2 in range(4):
            q0 = _aligned((r4 * 4 + r2) * l16, l16)
            rows = pl.ds(q0, l16)
            o, l = _attn_block(r16_ref[0, rows, :], r16_ref[1, rows, :], r16_ref[2, rows, :], bias3)
            dst = pl.ds(r4 * l4 + r2, l16, stride=4)
            o4_ref[dst, :] = o
            l4_ref[dst, :] = l
        return carry

    lax.fori_loop(0, 4, pat3, 0, unroll=4)

    def window(ref, i, q0, jb, nb):
        off = 0 if jb == 0 else (ATT_QB - ATT_KW if jb == nb - 1 else -ATT_SIDE)
        return ref[i, pl.ds(_aligned(q0 + off, ATT_SIDE), ATT_KW), :]

    def bias_of(jb, nb):
        return bias_ref[0 if jb == 0 else (2 if jb == nb - 1 else 1)]

    nb4 = l4 // ATT_QB

    def pat2(r, carry):
        for jb in range(nb4):
            q0 = _aligned(r * l4 + jb * ATT_QB, ATT_QB)
            rows = pl.ds(q0, ATT_QB)
            o, l = _attn_block(r4_ref[0, rows, :], window(r4_ref, 1, q0, jb, nb4),
                               window(r4_ref, 2, q0, jb, nb4), bias_of(jb, nb4))
            o, l = _merge(o, l, o4_ref[rows, :], l4_ref[rows, :])
            dst = pl.ds(4 * jb * ATT_QB + r, ATT_QB, stride=4)
            ot_ref[dst, :] = o
            lt_ref[dst, :] = l
        return carry

    lax.fori_loop(0, 4, pat2, 0, unroll=4)

    nb1 = seq // ATT_QB

    def pat1_block(n, jb):
        q0 = _aligned(n * ATT_QB, ATT_QB)
        rows = pl.ds(q0, ATT_QB)
        o, l = _attn_block(q_ref[0, rows, :], window(k_ref, 0, q0, jb, nb1),
                           window(v_ref, 0, q0, jb, nb1), bias_of(jb, nb1))
        o, _ = _merge(o, l, ot_ref[rows, :], lt_ref[rows, :])
        out_ref[0, rows, :] = o.astype(BF16)

    per_iter = (nb1 - 2) // 2

    def pat1(g, carry):
        for u in range(per_iter):
            pat1_block(per_iter * g + 1 + u, 1)
        return carry

    pat1_block(0, 0)
    lax.fori_loop(0, 2, pat1, 0, unroll=2)
    for n in range(2 * per_iter + 1, nb1):
        pat1_block(n, n)


def _dil_attn(aq, ak, av, *, seq):
    bsz = aq.shape[0]
    n_pairs = A_W // LANES
    blk = pl.BlockSpec((1, seq, LANES), lambda b, p: (b, 0, p))
    f32_slab = pltpu.VMEM((seq, LANES), F32)
    return pl.pallas_call(
        functools.partial(_dil_attn_kernel, seq=seq),
        grid=(bsz, n_pairs),
        in_specs=[blk, blk, blk],
        out_specs=blk,
        out_shape=jax.ShapeDtypeStruct((bsz, seq, A_W), BF16),
        scratch_shapes=[pltpu.VMEM((3, seq, LANES), F32),
                        pltpu.VMEM((3, seq, LANES), F32),
                        pltpu.VMEM((3, seq, LANES), BF16),
                        pltpu.VMEM((3, seq, LANES), BF16),
                        f32_slab, f32_slab, f32_slab, f32_slab,
                        pltpu.VMEM((4, 2 * ATT_QB, ATT_KW), F32)],
        compiler_params=_cparams(2),
        name="dil_attn",
    )(aq, ak, av)


def _layer(x, mod3, g_ffn1, w_gu1, w_down1, g_mix, w_in, gate_bias, g_q, g_k, g_mh, w_out,
           g_ffn2, w_gu2, w_down2, g_final):
    bsz, seq, d = x.shape
    t = bsz * seq
    tm = 512
    row2 = lambda v: v.reshape(1, -1).astype(F32)

    o_mk, o_mv = M_QKW, 2 * M_QKW
    o_g = o_mv + 2 * M_W
    o_aq = o_g + N_GATES

    def twice_per_head(w):
        wh = w.reshape(d, M_HEADS, 1, M_QK_DIM)
        return jnp.broadcast_to(wh, (d, M_HEADS, 2, M_QK_DIM)).reshape(d, M_QKD)

    w_cat = jnp.concatenate(
        [twice_per_head(w_in[:, :o_mk]), twice_per_head(w_in[:, o_mk:o_mv]), w_in[:, o_mv:o_g],
         w_in[:, o_aq:], w_in[:, o_g:o_aq],
         jnp.zeros((d, LANES - N_GATES), w_in.dtype)], axis=1).astype(BF16)
    gb = jnp.concatenate([gate_bias.reshape(1, N_GATES).astype(F32),
                          jnp.zeros((1, LANES - N_GATES), F32)], axis=1)
    gq = jnp.tile(row2(g_q), (1, A_HEADS))
    gk = jnp.tile(row2(g_k), (1, A_HEADS))
    x1, mq, mk, mv, mo, gates, aq, ak, av = _ffn(
        x.reshape(t, d), mod3, row2(g_ffn1), w_gu1.astype(BF16), w_down1.astype(BF16),
        seq=seq, mod_base=0, tm=tm,
        proj=(row2(g_mix), w_cat, gb, gq, gk, _rope_tables(seq)))

    to3 = lambda a: a.reshape(bsz, seq, a.shape[-1])
    nc = seq // M_CHUNK
    g5 = gates.reshape(bsz, nc, M_CHUNK, 2, 2, M_HEADS)
    g5 = g5.transpose(4, 0, 5, 1, 3, 2).reshape(2, bsz, M_HEADS * nc, 2 * M_CHUNK)
    hm = _mlstm(to3(mq), to3(mk), to3(mv), to3(mo), g5[0], g5[1], row2(g_mh), seq=seq)
    ha = _dil_attn(to3(aq), to3(ak), to3(av), seq=seq)

    w_out_b = w_out.astype(BF16)
    out = _ffn(x1, mod3, row2(g_ffn2), w_gu2.astype(BF16), w_down2.astype(BF16),
               seq=seq, mod_base=6, tm=tm,
               mix=(hm.reshape(t, M_W), ha.reshape(t, A_W), w_out_b[:M_W], w_out_b[M_W:]),
               gfin=row2(g_final))
    return out.reshape(bsz, seq, d)


def kernel(x, c, w_ada, b_ada, g_ffn1, w_gu1, w_down1, g_mix, w_in, gate_bias, g_q, g_k, g_mh,
           w_out, g_ffn2, w_gu2, w_down2, g_final):
    bsz, seq, d = x.shape
    depth = w_ada.shape[0]
    for l in range(depth):
        mod = _adaln(c, w_ada[l], b_ada[l].reshape(1, -1))
        mod3 = mod.reshape(bsz, N_MOD, d)
        x = _layer(x, mod3, g_ffn1[l], w_gu1[l], w_down1[l], g_mix[l], w_in[l], gate_bias[l],
                   g_q[l], g_k[l], g_mh[l], w_out[l], g_ffn2[l], w_gu2[l], w_down2[l], g_final[l])
    return x
```

```python
import functools

import jax
import jax.numpy as jnp
from jax import lax
from jax.experimental import pallas as pl
from jax.experimental.pallas import tpu as pltpu

F32 = jnp.float32
BF16 = jnp.bfloat16

D_MODEL = 1024
D_FF = 2816
N_MOD = 9
M_HEADS = 4
M_V_DIM = 128
M_QK_DIM = 64
M_CHUNK = 64
A_HEADS = 8
A_HEAD_DIM = 64
WINDOWS = (128, 512, 2048)
DILATIONS = (1, 4, 16)
ROT_DIM = A_HEAD_DIM // 4
ROPE_THETA = 500000.0
EPS = 1e-6
NEG_INF = -1e30
HALF_STEP = 0.5

LANES = 128
M_W = M_HEADS * M_V_DIM
M_QKW = M_HEADS * M_QK_DIM
A_W = A_HEADS * A_HEAD_DIM
N_GATES = 4 * M_HEADS
M_QKD = 2 * M_QKW
C_MQ, C_MK, C_MV, C_MO = 0, M_QKW, 2 * M_QKW, 2 * M_QKW + M_W
C_AQ = C_MO + M_W
C_AK = C_AQ + A_W
C_AV = C_AK + A_W
C_G = C_AV + A_W
IN_COLS_PAD = C_G + LANES

VMEM_LIMIT = 56 * 1024 * 1024


def _cparams(n_axes):
    return pltpu.CompilerParams(dimension_semantics=("arbitrary",) * n_axes,
                                vmem_limit_bytes=VMEM_LIMIT)


def _aligned(x, m):
    return x if isinstance(x, int) else pl.multiple_of(x, m)


def _rms(xf, g):
    ms = jnp.mean(xf * xf, axis=-1, keepdims=True)
    return xf * lax.rsqrt(ms + EPS) * g


def _adaln_kernel(c_ref, w_ref, b_ref, o_ref):
    c = c_ref[...]
    cs = (c * jax.nn.sigmoid(c)).astype(BF16)
    o_ref[...] = jnp.dot(cs, w_ref[...].astype(BF16), preferred_element_type=F32) + b_ref[...]


def _adaln(c, w, b):
    bsz, d = c.shape
    n = w.shape[1]
    tn = 1024
    return pl.pallas_call(
        _adaln_kernel,
        grid=(n // tn,),
        in_specs=[pl.BlockSpec((bsz, d), lambda j: (0, 0)),
                  pl.BlockSpec((d, tn), lambda j: (0, j)),
                  pl.BlockSpec((1, tn), lambda j: (0, j))],
        out_specs=pl.BlockSpec((bsz, tn), lambda j: (0, j)),
        out_shape=jax.ShapeDtypeStruct((bsz, n), F32),
        compiler_params=_cparams(1),
        name="adaln",
    )(c, w, b)


FFN_SUB_TILES = 2
N_PROJ_IN = 8
N_PROJ_OUT = 8
PROJ_GATE_OUT = 4


def _ffn_kernel(*refs, mod_base, with_mix, final_norm, with_proj):
    it = iter(refs)
    x_ref = next(it)
    if with_mix:
        hm_ref, ha_ref, wo_m_ref, wo_a_ref = next(it), next(it), next(it), next(it)
    mod_ref, g_ref, wgu_ref, wd_ref = next(it), next(it), next(it), next(it)
    if final_norm:
        gfin_ref = next(it)
    if with_proj:
        proj_in = [next(it) for _ in range(N_PROJ_IN)]
    o_ref = next(it)
    if with_proj:
        proj_out = [next(it) for _ in range(N_PROJ_OUT)]

    sh = mod_ref[0, mod_base:mod_base + 1, :]
    sc = mod_ref[0, mod_base + 1:mod_base + 2, :]
    gt = mod_ref[0, mod_base + 2:mod_base + 3, :]
    sub = x_ref.shape[0] // FFN_SUB_TILES
    for s in range(FFN_SUB_TILES):
        rows = pl.ds(s * sub, sub)
        x = x_ref[rows, :]
        if with_mix:
            gt_mix = mod_ref[0, mod_base - 1:mod_base, :]
            y = (jnp.dot(hm_ref[rows, :], wo_m_ref[...], preferred_element_type=F32)
                 + jnp.dot(ha_ref[rows, :], wo_a_ref[...], preferred_element_type=F32))
            x = x + gt_mix * y
        h = (_rms(x, g_ref[...]) * (1.0 + sc) + sh).astype(BF16)
        gu = jnp.dot(h, wgu_ref[...], preferred_element_type=F32)
        g = gu[:, :D_FF]
        u = gu[:, D_FF:]
        a = (g * jax.nn.sigmoid(g) * u).astype(BF16)
        y = jnp.dot(a, wd_ref[...], preferred_element_type=F32)
        x = x + HALF_STEP * gt * y
        if final_norm:
            x = _rms(x, gfin_ref[...])
        o_ref[rows, :] = x
        if with_proj:
            tables = [r.at[rows] for r in proj_in[N_PROJ_IN - 3:]]
            outs = [r.at[:, rows] if i == PROJ_GATE_OUT else r.at[rows] for i, r in enumerate(proj_out)]
            _mixer_in_proj(x, mod_ref, *proj_in[:N_PROJ_IN - 3], *tables, *outs)


def _ffn(x2d, mod3, g, wgu, wd, *, seq, mod_base, tm, mix=None, gfin=None, proj=None):
    t, d = x2d.shape
    with_mix = mix is not None
    final_norm = gfin is not None
    with_proj = proj is not None
    row = lambda i: (i, 0)
    const = lambda i: (0, 0)
    resident = pl.Buffered(1)
    args = [x2d]
    in_specs = [pl.BlockSpec((tm, d), row)]
    if with_mix:
        hm, ha, wo_m, wo_a = mix
        args += [hm, ha, wo_m, wo_a]
        in_specs += [pl.BlockSpec((tm, M_W), row), pl.BlockSpec((tm, A_W), row),
                     pl.BlockSpec((M_W, d), const, pipeline_mode=resident),
                     pl.BlockSpec((A_W, d), const, pipeline_mode=resident)]
    args += [mod3, g, wgu, wd]
    in_specs += [pl.BlockSpec((1, N_MOD, d), lambda i: ((i * tm) // seq, 0, 0)),
                 pl.BlockSpec((1, d), const),
                 pl.BlockSpec((d, 2 * D_FF), const, pipeline_mode=resident),
                 pl.BlockSpec((D_FF, d), const, pipeline_mode=resident)]
    if final_norm:
        args.append(gfin)
        in_specs.append(pl.BlockSpec((1, d), const))
    out_shape = [jax.ShapeDtypeStruct((t, d), F32)]
    if with_proj:
        g_mix, w_cat, gate_bias, gq, gk, tables = proj
        pos = lambda i: ((i * tm) % seq // tm, 0)
        args += [g_mix, w_cat, gate_bias, gq, gk, *tables]
        in_specs += [pl.BlockSpec((1, d), const),
                     pl.BlockSpec((d, IN_COLS_PAD), const, pipeline_mode=resident),
                     pl.BlockSpec((1, LANES), const),
                     pl.BlockSpec((1, A_W), const),
                     pl.BlockSpec((1, A_W), const),
                     pl.BlockSpec((tm, LANES), pos),
                     pl.BlockSpec((tm, LANES), pos),
                     pl.BlockSpec((tm, LANES), pos)]
        bf = lambda n: jax.ShapeDtypeStruct((t, n), BF16)
        out_shape += [bf(M_QKW), bf(M_QKW), bf(M_W), bf(M_W),
                      jax.ShapeDtypeStruct((N_GATES, t), F32), bf(A_W), bf(A_W), bf(A_W)]
    out_specs = [pl.BlockSpec((tm, s.shape[1]), row) for s in out_shape]
    if with_proj:
        out_specs[1 + PROJ_GATE_OUT] = pl.BlockSpec((N_GATES, tm), lambda i: (0, i))
    outs = pl.pallas_call(
        functools.partial(_ffn_kernel, mod_base=mod_base, with_mix=with_mix, final_norm=final_norm,
                          with_proj=with_proj),
        grid=(t // tm,),
        in_specs=in_specs,
        out_specs=out_specs,
        out_shape=out_shape,
        compiler_params=_cparams(1),
        name="ffn_mix" if with_mix else ("ffn_proj" if with_proj else "ffn"),
    )(*args)
    return outs if with_proj else outs[0]


def _group_norm_rope(a, gain, cos_t, sin_a, sin_b, scale):
    lane = lax.broadcasted_iota(jnp.int32, (1, LANES), 1)
    first = lane < A_HEAD_DIM
    outs = []
    for p in range(A_W // LANES):
        blk = a[:, p * LANES:(p + 1) * LANES]
        sq = blk * blk
        s0 = jnp.sum(jnp.where(first, sq, 0.0), axis=-1, keepdims=True)
        s1 = jnp.sum(jnp.where(first, 0.0, sq), axis=-1, keepdims=True)
        r = jnp.where(first, lax.rsqrt(s0 * (1.0 / A_HEAD_DIM) + EPS),
                      lax.rsqrt(s1 * (1.0 / A_HEAD_DIM) + EPS))
        y = blk * r * gain[:, p * LANES:(p + 1) * LANES]
        half = ROT_DIM // 2
        y = (y * cos_t + pltpu.roll(y, LANES - half, 1) * sin_a + pltpu.roll(y, half, 1) * sin_b)
        outs.append((y * scale).astype(BF16))
    return jnp.concatenate(outs, axis=-1)


def _mixer_in_proj(x, mod_ref, g_ref, w_ref, gb_ref, gq_ref, gk_ref, cos_ref, sa_ref, sb_ref,
                   mq_ref, mk_ref, mv_ref, mo_ref, gate_ref, aq_ref, ak_ref, av_ref):
    sh = mod_ref[0, 3:4, :]
    sc = mod_ref[0, 4:5, :]
    h = (_rms(x, g_ref[...]) * (1.0 + sc) + sh).astype(BF16)
    proj = jnp.dot(h, w_ref[...], preferred_element_type=F32)

    mq_ref[...] = proj[:, C_MQ:C_MK].astype(BF16)
    mk_ref[...] = (proj[:, C_MK:C_MV] * (M_QK_DIM ** -0.5)).astype(BF16)
    mv_ref[...] = proj[:, C_MV:C_MO].astype(BF16)
    mo_ref[...] = jax.nn.sigmoid(proj[:, C_MO:C_AQ]).astype(BF16)
    av_ref[...] = proj[:, C_AV:C_G].astype(BF16)

    gb = proj[:, C_G:C_G + LANES] + gb_ref[...]
    lane = lax.broadcasted_iota(jnp.int32, (1, LANES), 1)
    is_forget = (lane % (2 * M_HEADS)) >= M_HEADS
    log_sig = jnp.minimum(gb, 0.0) - jnp.log(1.0 + jnp.exp(-jnp.abs(gb)))
    gate_ref[...] = jnp.where(is_forget, log_sig, gb).T[:N_GATES, :]

    cos_t, sin_a, sin_b = cos_ref[...], sa_ref[...], sb_ref[...]
    aq_ref[...] = _group_norm_rope(proj[:, C_AQ:C_AK], gq_ref[...], cos_t, sin_a, sin_b,
                                   A_HEAD_DIM ** -0.5 * LOG2E)
    ak_ref[...] = _group_norm_rope(proj[:, C_AK:C_AV], gk_ref[...], cos_t, sin_a, sin_b, 1.0)


def _rope_tables(seq):
    half = ROT_DIM // 2
    inv_freq = ROPE_THETA ** (-2.0 * jnp.arange(half, dtype=F32) / ROT_DIM)
    ang = jnp.arange(seq, dtype=F32)[:, None] * inv_freq[None, :]
    cos, sin = jnp.cos(ang), jnp.sin(ang)
    ones = jnp.ones((seq, A_HEAD_DIM - ROT_DIM), F32)
    zeros_h = jnp.zeros((seq, half), F32)
    zeros_r = jnp.zeros((seq, A_HEAD_DIM - ROT_DIM), F32)
    cos_t = jnp.concatenate([cos, cos, ones], axis=-1)
    sin_a = jnp.concatenate([-sin, zeros_h, zeros_r], axis=-1)
    sin_b = jnp.concatenate([zeros_h, sin, zeros_r], axis=-1)
    tile2 = lambda t: jnp.concatenate([t, t], axis=-1)
    return tile2(cos_t), tile2(sin_a), tile2(sin_b)


def _split3(x):
    hi = x.astype(BF16)
    r1 = x - hi.astype(F32)
    mid = r1.astype(BF16)
    lo = (r1 - mid.astype(F32)).astype(BF16)
    return hi, mid, lo


M_TILE = 16
C_ROWS = M_V_DIM + M_TILE
R_ALPHA, R_BETA, R_WI, R_EMT, R_WG, R_MM = range(6)


def _seg_scan(x, op, fill):
    lane = lax.broadcasted_iota(jnp.int32, (1, LANES), 1)
    first = lane < M_CHUNK
    pos = lane % M_CHUNK
    k = 1
    while k < M_CHUNK:
        from_left = jnp.where(pos >= k, pltpu.roll(x, k, 1), fill)
        from_right = jnp.where(pos < M_CHUNK - k, pltpu.roll(x, LANES - k, 1), fill)
        x = op(x, jnp.where(first, from_left, from_right))
        k *= 2
    return x


def _half_reduce(x, red, fill):
    first = lax.broadcasted_iota(jnp.int32, (1, LANES), 1) < M_CHUNK
    a = red(jnp.where(first, x, fill), axis=-1, keepdims=True)
    b = red(jnp.where(first, fill, x), axis=-1, keepdims=True)
    return jnp.where(first, a, b)


def _mlstm_kernel(q_ref, k_ref, v_ref, og_ref, gi_ref, gf_ref, gmh_ref, out_ref,
                  bt_ref, mg_ref, ms_ref, mn_ref, rows_ref, ut_ref, cst_ref, cs_ref, qd_ref, kd_ref,
                  *, seq):
    lc = M_CHUNK
    nc = seq // lc
    nrow = M_HEADS * nc
    dn_t = (((0,), (0,)), ((), ()))
    dn_nt = (((1,), (1,)), ((), ()))
    lane = lax.broadcasted_iota(jnp.int32, (1, LANES), 1)
    first = lane < lc

    for src_ref, dst_ref in ((q_ref, qd_ref), (k_ref, kd_ref)):
        for p in range(M_HEADS // 2):
            x = src_ref[0, :, p * LANES:(p + 1) * LANES].astype(F32)
            swapped = pltpu.roll(x, lc, 1)
            dst_ref[:, (2 * p) * LANES:(2 * p + 1) * LANES] = jnp.where(first, x, swapped).astype(BF16)
            dst_ref[:, (2 * p + 1) * LANES:(2 * p + 2) * LANES] = jnp.where(first, swapped, x).astype(BF16)

    gi = gi_ref[0]
    gf = gf_ref[0]
    b2 = _seg_scan(gf, jnp.add, 0.0)
    u2 = gi - b2
    cmax2 = _seg_scan(u2, jnp.maximum, NEG_INF)
    btot2 = _half_reduce(gf, jnp.sum, 0.0)
    maxu2 = _half_reduce(u2, jnp.max, NEG_INF)
    mg2 = btot2 + maxu2
    bt_ref[...] = btot2
    mg_ref[...] = mg2

    m = jnp.full((M_HEADS, LANES), NEG_INF, F32)
    for i in range(nc):
        rows_f = pl.ds(i, M_HEADS, stride=nc)
        rows_b = pl.ds(nc - 1 - i, M_HEADS, stride=nc)
        bt = jnp.where(first, bt_ref[rows_f, :], bt_ref[rows_b, :])
        mg = jnp.where(first, mg_ref[rows_f, :], mg_ref[rows_b, :])
        m_new = jnp.maximum(bt + m, mg)
        ms_ref[0, rows_f, :] = m
        ms_ref[1, rows_b, :] = m
        mn_ref[0, rows_f, :] = m_new
        mn_ref[1, rows_b, :] = m_new
        m = m_new
    mstart2 = jnp.where(first, ms_ref[0], ms_ref[1])
    mnext2 = jnp.where(first, mn_ref[0], mn_ref[1])

    mm2 = jnp.maximum(mstart2, cmax2)
    rows_ref[R_ALPHA] = jnp.exp(btot2 + mstart2 - mnext2)
    rows_ref[R_BETA] = jnp.exp(mg2 - mnext2)
    rows_ref[R_WI] = jnp.exp(mstart2 - mm2)
    rows_ref[R_EMT] = jnp.exp(-(b2 + mm2))
    rows_ref[R_WG] = jnp.exp(u2 - maxu2)
    rows_ref[R_MM] = mm2

    ut_ref[...] = jnp.zeros(ut_ref.shape, F32)
    for piece, up in enumerate(_split3(u2)):
        up = up.astype(F32)
        by_dir = (up, pltpu.roll(up, lc, 1))
        for h in range(M_HEADS):
            for direction in range(2):
                row = (h % 2) * 6 + direction * 3 + piece
                ut_ref[pl.ds((h // 2) * nc * M_TILE + row, nc, stride=M_TILE), :] = (
                    by_dir[direction][h * nc:(h + 1) * nc])

    cst_ref[...] = jnp.zeros(cst_ref.shape, F32)
    eye = (lax.broadcasted_iota(jnp.int32, (LANES, LANES), 0)
           == lax.broadcasted_iota(jnp.int32, (LANES, LANES), 1))

    def step(i, carry):
        cf = i
        cb = nc - 1 - i
        rf = _aligned(cf * lc, lc)
        rb = _aligned(cb * lc, lc)
        for h in range(M_HEADS):
            vs = slice(h * M_V_DIM, (h + 1) * M_V_DIM)
            st = cst_ref[h]
            st_b = st.astype(BF16)
            l0 = (h % 2) * LANES
            cs_ref[h // 2, cf, :, l0:l0 + lc] = st_b[:, 0:lc]
            cs_ref[h // 2, cb, :, l0 + lc:l0 + LANES] = st_b[:, lc:LANES]
            row_f = h * nc + cf
            row_b = h * nc + cb
            pick = lambda j: jnp.where(first, rows_ref[j, pl.ds(row_f, 1), :],
                                       rows_ref[j, pl.ds(row_b, 1), :])
            alpha, beta, wg = pick(R_ALPHA), pick(R_BETA), pick(R_WG)
            k_f = kd_ref[pl.ds(rf, lc), vs]
            k_b = kd_ref[pl.ds(rb, lc), vs]
            zero = jnp.zeros_like(k_f)
            k_bd = jnp.concatenate([jnp.where(first, k_f, zero), jnp.where(first, zero, k_b)], axis=0)
            diag = jnp.where(eye, wg, 0.0).astype(BF16)
            kw = jnp.dot(diag, k_bd, preferred_element_type=F32)
            v_st = jnp.concatenate([v_ref[0, pl.ds(rf, lc), vs], v_ref[0, pl.ds(rb, lc), vs]], axis=0)
            upd = lax.dot_general(v_st, kw.astype(BF16), dn_t, preferred_element_type=F32)
            n_upd = jnp.sum(kw, axis=0, keepdims=True)
            cst_ref[h, 0:M_V_DIM, :] = alpha * st[0:M_V_DIM] + beta * upd
            cst_ref[h, M_V_DIM:M_V_DIM + 1, :] = alpha * st[M_V_DIM:M_V_DIM + 1] + beta * n_upd
        return carry

    lax.fori_loop(0, nc, step, 0, unroll=8)

    pw = 2 * LANES
    lane2 = lax.broadcasted_iota(jnp.int32, (1, pw), 1)
    blk2 = lane2 // lc
    s_i = lax.broadcasted_iota(jnp.int32, (lc, pw), 0)
    t_i = lax.broadcasted_iota(jnp.int32, (lc, pw), 1) % lc
    causal = jnp.where((s_i - t_i) * jnp.where(blk2 % 2 == 0, 1, -1) <= 0, 0.0, NEG_INF).astype(F32)
    tile_row = lax.broadcasted_iota(jnp.int32, (M_TILE, pw), 0)
    spread = jnp.where(tile_row // 3 == lax.broadcasted_iota(jnp.int32, (M_TILE, pw), 1) // lc,
                       1.0, 0.0).astype(BF16)

    def pair_row(j, p, c):
        return jnp.concatenate([rows_ref[j, pl.ds((2 * p) * nc + c, 1), :],
                                rows_ref[j, pl.ds((2 * p + 1) * nc + c, 1), :]], axis=1)

    def chunk(c, carry):
        r0 = _aligned(c * lc, lc)
        rows = pl.ds(r0, lc)
        for p in range(M_HEADS // 2):
            ps = slice(p * pw, (p + 1) * pw)
            k_d = kd_ref[rows, ps]
            q_d = qd_ref[rows, ps]
            zero = jnp.zeros_like(q_d)
            q_bd = jnp.concatenate([jnp.where(blk2 == b, q_d, zero) for b in range(4)], axis=0)
            both = lax.dot_general(jnp.concatenate([k_d, cs_ref[p, c]], axis=0), q_bd, dn_nt,
                                   preferred_element_type=F32)
            qk_t = both[0:lc]
            inter = both[lc:]
            tile = pl.ds(_aligned((p * nc + c) * M_TILE, M_TILE), M_TILE)
            u_rows = lax.dot_general(ut_ref[tile, 0:lc].astype(BF16), spread, dn_t,
                                     preferred_element_type=F32)
            dm = u_rows - pair_row(R_MM, p, c) + causal
            s_t = jnp.exp(dm) * qk_t
            den_in = jnp.sum(s_t, axis=0, keepdims=True)
            s_b = s_t.astype(BF16)
            zero_s = jnp.zeros_like(s_b)
            s_bd = jnp.concatenate([jnp.where(lane2 < LANES, s_b, zero_s),
                                    jnp.where(lane2 < LANES, zero_s, s_b)], axis=0)
            v_st = jnp.concatenate([v_ref[0, rows, (2 * p) * M_V_DIM:(2 * p + 1) * M_V_DIM],
                                    v_ref[0, rows, (2 * p + 1) * M_V_DIM:(2 * p + 2) * M_V_DIM]], axis=0)
            num = lax.dot_general(v_st, s_bd, dn_t, preferred_element_type=F32)
            wi = pair_row(R_WI, p, c)
            num = num + wi * inter[0:M_V_DIM]
            den = den_in + wi * inter[M_V_DIM:M_V_DIM + 1]
            h_t = num * (1.0 / jnp.maximum(jnp.abs(den), pair_row(R_EMT, p, c)))
            for hh in range(2):
                vs = slice((2 * p + hh) * M_V_DIM, (2 * p + hh + 1) * M_V_DIM)
                h_n = h_t[:, hh * LANES:(hh + 1) * LANES].T
                hs = h_n[0:lc] + h_n[lc:]
                y = _rms(hs, gmh_ref[:, vs]) * og_ref[0, rows, vs].astype(F32)
                out_ref[0, rows, vs] = y.astype(BF16)
        return carry

    lax.fori_loop(0, nc, chunk, 0, unroll=4)


def _mlstm(mqd, mkd, mv, mo, gi2, gf2, gmh, *, seq):
    bsz = mqd.shape[0]
    nc = seq // M_CHUNK
    nrow = M_HEADS * nc
    assert 2 * M_CHUNK == LANES and 2 * M_QK_DIM == LANES and M_V_DIM == LANES
    b3 = lambda b: (b, 0, 0)
    slab = pltpu.VMEM((nrow, LANES), F32)
    return pl.pallas_call(
        functools.partial(_mlstm_kernel, seq=seq),
        grid=(bsz,),
        in_specs=[pl.BlockSpec((1, seq, M_QKW), b3),
                  pl.BlockSpec((1, seq, M_QKW), b3),
                  pl.BlockSpec((1, seq, M_W), b3),
                  pl.BlockSpec((1, seq, M_W), b3),
                  pl.BlockSpec((1, nrow, LANES), b3),
                  pl.BlockSpec((1, nrow, LANES), b3),
                  pl.BlockSpec((1, M_W), lambda b: (0, 0))],
        out_specs=pl.BlockSpec((1, seq, M_W), b3),
        out_shape=jax.ShapeDtypeStruct((bsz, seq, M_W), BF16),
        scratch_shapes=[slab, slab,
                        pltpu.VMEM((2, nrow, LANES), F32),
                        pltpu.VMEM((2, nrow, LANES), F32),
                        pltpu.VMEM((6, nrow, LANES), F32),
                        pltpu.VMEM((M_HEADS // 2 * nc * M_TILE, LANES), F32),
                        pltpu.VMEM((M_HEADS, C_ROWS, LANES), F32),
                        pltpu.VMEM((M_HEADS // 2, nc, C_ROWS, 2 * LANES), BF16),
                        pltpu.VMEM((seq, M_QKD), BF16),
                        pltpu.VMEM((seq, M_QKD), BF16)],
        compiler_params=_cparams(1),
        name="mlstm",
    )(mqd, mkd, mv, mo, gi2, gf2, gmh)


ATT_QB = 128
ATT_SIDE = 64
ATT_KW = ATT_QB + 2 * ATT_SIDE
LOG2E = 1.4426950408889634
assert all(w // (2 * d) == ATT_SIDE for w, d in zip(WINDOWS, DILATIONS))
assert DILATIONS == (1, 4, 16)


def _band_bias(kw, offset):
    qi = lax.broadcasted_iota(jnp.int32, (2 * ATT_QB, kw), 0) % ATT_QB
    ki = lax.broadcasted_iota(jnp.int32, (2 * ATT_QB, kw), 1)
    return jnp.where(jnp.abs(ki + offset - qi) <= ATT_SIDE, 0.0, NEG_INF).astype(F32)


def _attn_block(qb, kb, vb, bias):
    first = lax.broadcasted_iota(jnp.int32, (1, LANES), 1) < A_HEAD_DIM
    zero = jnp.zeros_like(qb)
    qm = jnp.concatenate([jnp.where(first, qb, zero), jnp.where(first, zero, qb)], axis=0)
    s = lax.dot_general(qm, kb, (((1,), (1,)), ((), ())), preferred_element_type=F32) + bias
    m = jnp.max(s, axis=-1, keepdims=True)
    p = jnp.exp2(s - m)
    den = jnp.sum(p, axis=-1, keepdims=True)
    o = jnp.dot(p.astype(BF16), vb, preferred_element_type=F32) / den
    lse = m + jnp.log2(den)
    return (jnp.where(first, o[:ATT_QB], o[ATT_QB:]),
            jnp.where(first, lse[:ATT_QB], lse[ATT_QB:]))


def _merge(o_a, l_a, o_b, l_b):
    top = jnp.maximum(l_a, l_b)
    w_a = jnp.exp2(l_a - top)
    w_b = jnp.exp2(l_b - top)
    tot = w_a + w_b
    return (w_a * o_a + w_b * o_b) / tot, top + jnp.log2(tot)


def _dil_attn_kernel(q_ref, k_ref, v_ref, out_ref, tok_ref, r4f_ref, r4_ref, r16_ref,
                     o4_ref, l4_ref, ot_ref, lt_ref, bias_ref, *, seq):
    l4 = seq // 4
    l16 = seq // 16
    n_in = 3

    @pl.when((pl.program_id(0) == 0) & (pl.program_id(1) == 0))
    def _():
        bias_ref[0] = _band_bias(ATT_KW, 0)
        bias_ref[1] = _band_bias(ATT_KW, -ATT_SIDE)
        bias_ref[2] = _band_bias(ATT_KW, ATT_QB - ATT_KW)
        bias_ref[3, :, :ATT_QB] = _band_bias(ATT_QB, 0)

    for i, ref in enumerate((q_ref, k_ref, v_ref)):
        tok_ref[i] = ref[0].astype(F32)
    for i in range(n_in):
        for r in range(4):
            x = tok_ref[i, pl.ds(r, l4, stride=4), :]
            r4f_ref[i, r * l4:(r + 1) * l4, :] = x
            r4_ref[i, r * l4:(r + 1) * l4, :] = x.astype(BF16)
    for i in range(n_in):
        for c in range(16):
            x = r4f_ref[i, pl.ds((c // 4) * l4 + c % 4, l16, stride=4), :]
            r16_ref[i, c * l16:(c + 1) * l16, :] = x.astype(BF16)

    bias3 = bias_ref[3, :, :ATT_QB]

    def pat3(r4, carry):
        for r2 in range(4):
            q0 = _aligned((r4 * 4 + r2) * l16, l16)
            rows = pl.ds(q0, l16)
            o, l = _attn_block(r16_ref[0, rows, :], r16_ref[1, rows, :], r16_ref[2, rows, :], bias3)
            dst = pl.ds(r4 * l4 + r2, l16, stride=4)
            o4_ref[dst, :] = o
            l4_ref[dst, :] = l
        return carry

    lax.fori_loop(0, 4, pat3, 0, unroll=4)

    def window(ref, i, q0, jb, nb):
        off = 0 if jb == 0 else (ATT_QB - ATT_KW if jb == nb - 1 else -ATT_SIDE)
        return ref[i, pl.ds(_aligned(q0 + off, ATT_SIDE), ATT_KW), :]

    def bias_of(jb, nb):
        return bias_ref[0 if jb == 0 else (2 if jb == nb - 1 else 1)]

    nb4 = l4 // ATT_QB

    def pat2(r, carry):
        for jb in range(nb4):
            q0 = _aligned(r * l4 + jb * ATT_QB, ATT_QB)
            rows = pl.ds(q0, ATT_QB)
            o, l = _attn_block(r4_ref[0, rows, :], window(r4_ref, 1, q0, jb, nb4),
                               window(r4_ref, 2, q0, jb, nb4), bias_of(jb, nb4))
            o, l = _merge(o, l, o4_ref[rows, :], l4_ref[rows, :])
            dst = pl.ds(4 * jb * ATT_QB + r, ATT_QB, stride=4)
            ot_ref[dst, :] = o
            lt_ref[dst, :] = l
        return carry

    lax.fori_loop(0, 4, pat2, 0, unroll=4)

    nb1 = seq // ATT_QB

    def pat1_block(n, jb):
        q0 = _aligned(n * ATT_QB, ATT_QB)
        rows = pl.ds(q0, ATT_QB)
        o, l = _attn_block(q_ref[0, rows, :], window(k_ref, 0, q0, jb, nb1),
                           window(v_ref, 0, q0, jb, nb1), bias_of(jb, nb1))
        o, _ = _merge(o, l, ot_ref[rows, :], lt_ref[rows, :])
        out_ref[0, rows, :] = o.astype(BF16)

    per_iter = (nb1 - 2) // 2

    def pat1(g, carry):
        for u in range(per_iter):
            pat1_block(per_iter * g + 1 + u, 1)
        return carry

    pat1_block(0, 0)
    lax.fori_loop(0, 2, pat1, 0, unroll=2)
    for n in range(2 * per_iter + 1, nb1):
        pat1_block(n, n)


def _dil_attn(aq, ak, av, *, seq):
    bsz = aq.shape[0]
    n_pairs = A_W // LANES
    blk = pl.BlockSpec((1, seq, LANES), lambda b, p: (b, 0, p))
    f32_slab = pltpu.VMEM((seq, LANES), F32)
    return pl.pallas_call(
        functools.partial(_dil_attn_kernel, seq=seq),
        grid=(bsz, n_pairs),
        in_specs=[blk, blk, blk],
        out_specs=blk,
        out_shape=jax.ShapeDtypeStruct((bsz, seq, A_W), BF16),
        scratch_shapes=[pltpu.VMEM((3, seq, LANES), F32),
                        pltpu.VMEM((3, seq, LANES), F32),
                        pltpu.VMEM((3, seq, LANES), BF16),
                        pltpu.VMEM((3, seq, LANES), BF16),
                        f32_slab, f32_slab, f32_slab, f32_slab,
                        pltpu.VMEM((4, 2 * ATT_QB, ATT_KW), F32)],
        compiler_params=_cparams(2),
        name="dil_attn",
    )(aq, ak, av)


def _layer(x, mod3, g_ffn1, w_gu1, w_down1, g_mix, w_in, gate_bias, g_q, g_k, g_mh, w_out,
           g_ffn2, w_gu2, w_down2, g_final):
    bsz, seq, d = x.shape
    t = bsz * seq
    tm = 512
    row2 = lambda v: v.reshape(1, -1).astype(F32)

    o_g = 2 * M_QKW + 2 * M_W
    o_aq = o_g + N_GATES
    w_cat = jnp.concatenate(
        [w_in[:, :o_g], w_in[:, o_aq:], w_in[:, o_g:o_aq],
         jnp.zeros((d, LANES - N_GATES), w_in.dtype)], axis=1).astype(BF16)
    gb = jnp.concatenate([gate_bias.reshape(1, N_GATES).astype(F32),
                          jnp.zeros((1, LANES - N_GATES), F32)], axis=1)
    gq = jnp.tile(row2(g_q), (1, A_HEADS))
    gk = jnp.tile(row2(g_k), (1, A_HEADS))
    x1, mq, mk, mv, mo, gates, aq, ak, av = _ffn(
        x.reshape(t, d), mod3, row2(g_ffn1), w_gu1.astype(BF16), w_down1.astype(BF16),
        seq=seq, mod_base=0, tm=tm,
        proj=(row2(g_mix), w_cat, gb, gq, gk, _rope_tables(seq)))

    to3 = lambda a: a.reshape(bsz, seq, a.shape[-1])
    nc = seq // M_CHUNK
    g5 = gates.reshape(2, 2, M_HEADS, bsz, nc, M_CHUNK)
    g5 = g5.transpose(1, 3, 2, 4, 0, 5).reshape(2, bsz, M_HEADS * nc, 2 * M_CHUNK)
    hm = _mlstm(to3(mq), to3(mk), to3(mv), to3(mo), g5[0], g5[1], row2(g_mh), seq=seq)
    ha = _dil_attn(to3(aq), to3(ak), to3(av), seq=seq)

    w_out_b = w_out.astype(BF16)
    out = _ffn(x1, mod3, row2(g_ffn2), w_gu2.astype(BF16), w_down2.astype(BF16),
               seq=seq, mod_base=6, tm=tm,
               mix=(hm.reshape(t, M_W), ha.reshape(t, A_W), w_out_b[:M_W], w_out_b[M_W:]),
               gfin=row2(g_final))
    return out.reshape(bsz, seq, d)


def kernel(x, c, w_ada, b_ada, g_ffn1, w_gu1, w_down1, g_mix, w_in, gate_bias, g_q, g_k, g_mh,
           w_out, g_ffn2, w_gu2, w_down2, g_final):
    bsz, seq, d = x.shape
    depth = w_ada.shape[0]
    for l in range(depth):
        mod = _adaln(c, w_ada[l], b_ada[l].reshape(1, -1))
        mod3 = mod.reshape(bsz, N_MOD, d)
        x = _layer(x, mod3, g_ffn1[l], w_gu1[l], w_down1[l], g_mix[l], w_in[l], gate_bias[l],
                   g_q[l], g_k[l], g_mh[l], w_out[l], g_ffn2[l], w_gu2[l], w_down2[l], g_final[l])
    return x
```

```python
import functools

import jax
import jax.numpy as jnp
import numpy as np
from jax import lax
from jax.experimental import pallas as pl
from jax.experimental.pallas import tpu as pltpu

F32 = jnp.float32
BF16 = jnp.bfloat16

D_MODEL = 1024
D_FF = 2816
N_MOD = 9
M_HEADS = 4
M_V_DIM = 128
M_QK_DIM = 64
M_CHUNK = 64
A_HEADS = 8
A_HEAD_DIM = 64
WINDOWS = (128, 512, 2048)
DILATIONS = (1, 4, 16)
ROT_DIM = A_HEAD_DIM // 4
ROPE_THETA = 500000.0
EPS = 1e-6
NEG_INF = -1e30
HALF_STEP = 0.5

LANES = 128
M_W = M_HEADS * M_V_DIM
M_QKW = M_HEADS * M_QK_DIM
A_W = A_HEADS * A_HEAD_DIM
N_GATES = 4 * M_HEADS
M_PROJ_W = 2 * M_QKW + 2 * M_W
A_PROJ_W = 3 * A_W

VMEM_LIMIT = 56 * 1024 * 1024


def _cparams(n_axes):
    return pltpu.CompilerParams(dimension_semantics=("arbitrary",) * n_axes,
                                vmem_limit_bytes=VMEM_LIMIT)


def _aligned(x, m):
    return x if isinstance(x, int) else pl.multiple_of(x, m)


def _rms(xf, g):
    ms = jnp.mean(xf * xf, axis=-1, keepdims=True)
    return xf * lax.rsqrt(ms + EPS) * g


def _adaln_kernel(c_ref, w_ref, b_ref, o_ref):
    c = c_ref[...]
    cs = (c * jax.nn.sigmoid(c)).astype(BF16)
    o_ref[...] = jnp.dot(cs, w_ref[...].astype(BF16), preferred_element_type=F32) + b_ref[...]


def _adaln(c, w, b):
    bsz, d = c.shape
    n = w.shape[1]
    tn = 1024
    return pl.pallas_call(
        _adaln_kernel,
        grid=(n // tn,),
        in_specs=[pl.BlockSpec((bsz, d), lambda j: (0, 0)),
                  pl.BlockSpec((d, tn), lambda j: (0, j)),
                  pl.BlockSpec((1, tn), lambda j: (0, j))],
        out_specs=pl.BlockSpec((bsz, tn), lambda j: (0, j)),
        out_shape=jax.ShapeDtypeStruct((bsz, n), F32),
        compiler_params=_cparams(1),
        name="adaln",
    )(c, w, b)


FFN_SUB_TILES = 2
N_PROJ_IN = 10
N_PROJ_OUT = 8
PROJ_GATE_OUT = 4


def _ffn_kernel(*refs, mod_base, with_mix, final_norm, with_proj):
    it = iter(refs)
    x_ref = next(it)
    if with_mix:
        hm_ref, ha_ref, wo_m_ref, wo_a_ref = next(it), next(it), next(it), next(it)
    mod_ref, g_ref, wgu_ref, wd_ref = next(it), next(it), next(it), next(it)
    if final_norm:
        gfin_ref = next(it)
    if with_proj:
        proj_in = [next(it) for _ in range(N_PROJ_IN)]
    o_ref = next(it)
    if with_proj:
        proj_out = [next(it) for _ in range(N_PROJ_OUT)]

    sh = mod_ref[0, mod_base:mod_base + 1, :]
    sc = mod_ref[0, mod_base + 1:mod_base + 2, :]
    gt = mod_ref[0, mod_base + 2:mod_base + 3, :]
    sub = x_ref.shape[0] // FFN_SUB_TILES
    for s in range(FFN_SUB_TILES):
        rows = pl.ds(s * sub, sub)
        x = x_ref[rows, :]
        if with_mix:
            gt_mix = mod_ref[0, mod_base - 1:mod_base, :]
            y = (jnp.dot(hm_ref[rows, :], wo_m_ref[...], preferred_element_type=F32)
                 + jnp.dot(ha_ref[rows, :], wo_a_ref[...], preferred_element_type=F32))
            x = x + gt_mix * y
        h = (_rms(x, g_ref[...]) * (1.0 + sc) + sh).astype(BF16)
        gu = jnp.dot(h, wgu_ref[...], preferred_element_type=F32)
        g = gu[:, :D_FF]
        u = gu[:, D_FF:]
        a = (g * jax.nn.sigmoid(g) * u).astype(BF16)
        y = jnp.dot(a, wd_ref[...], preferred_element_type=F32)
        x = x + HALF_STEP * gt * y
        if final_norm:
            x = _rms(x, gfin_ref[...])
        o_ref[rows, :] = x
        if with_proj:
            tables = [r.at[rows] for r in proj_in[N_PROJ_IN - 3:]]
            outs = [r.at[:, rows] if i == PROJ_GATE_OUT else r.at[rows] for i, r in enumerate(proj_out)]
            _mixer_in_proj(x, mod_ref, *proj_in[:N_PROJ_IN - 3], *tables, *outs)


def _ffn(x2d, mod3, g, wgu, wd, *, seq, mod_base, tm, mix=None, gfin=None, proj=None):
    t, d = x2d.shape
    with_mix = mix is not None
    final_norm = gfin is not None
    with_proj = proj is not None
    row = lambda i: (i, 0)
    const = lambda i: (0, 0)
    resident = pl.Buffered(1)
    args = [x2d]
    in_specs = [pl.BlockSpec((tm, d), row)]
    if with_mix:
        hm, ha, wo_m, wo_a = mix
        args += [hm, ha, wo_m, wo_a]
        in_specs += [pl.BlockSpec((tm, M_W), row), pl.BlockSpec((tm, A_W), row),
                     pl.BlockSpec((M_W, d), const, pipeline_mode=resident),
                     pl.BlockSpec((A_W, d), const, pipeline_mode=resident)]
    args += [mod3, g, wgu, wd]
    in_specs += [pl.BlockSpec((1, N_MOD, d), lambda i: ((i * tm) // seq, 0, 0)),
                 pl.BlockSpec((1, d), const),
                 pl.BlockSpec((d, 2 * D_FF), const, pipeline_mode=resident),
                 pl.BlockSpec((D_FF, d), const, pipeline_mode=resident)]
    if final_norm:
        args.append(gfin)
        in_specs.append(pl.BlockSpec((1, d), const))
    out_shape = [jax.ShapeDtypeStruct((t, d), F32)]
    if with_proj:
        g_mix, w_m, w_g, w_a, gate_bias, gq, gk, tables = proj
        pos = lambda i: ((i * tm) % seq // tm, 0)
        args += [g_mix, w_m, w_g, w_a, gate_bias, gq, gk, *tables]
        in_specs += [pl.BlockSpec((1, d), const),
                     pl.BlockSpec((d, M_PROJ_W), const, pipeline_mode=resident),
                     pl.BlockSpec((d, LANES), const, pipeline_mode=resident),
                     pl.BlockSpec((d, A_PROJ_W), const, pipeline_mode=resident),
                     pl.BlockSpec((1, LANES), const),
                     pl.BlockSpec((1, A_W), const),
                     pl.BlockSpec((1, A_W), const),
                     pl.BlockSpec((tm, LANES), pos),
                     pl.BlockSpec((tm, LANES), pos),
                     pl.BlockSpec((tm, LANES), pos)]
        bf = lambda n: jax.ShapeDtypeStruct((t, n), BF16)
        out_shape += [bf(M_QKW), bf(M_QKW), bf(M_W), bf(M_W),
                      jax.ShapeDtypeStruct((N_GATES, t), F32), bf(A_W), bf(A_W), bf(A_W)]
    out_specs = [pl.BlockSpec((tm, s.shape[1]), row) for s in out_shape]
    if with_proj:
        out_specs[1 + PROJ_GATE_OUT] = pl.BlockSpec((N_GATES, tm), lambda i: (0, i))
    outs = pl.pallas_call(
        functools.partial(_ffn_kernel, mod_base=mod_base, with_mix=with_mix, final_norm=final_norm,
                          with_proj=with_proj),
        grid=(t // tm,),
        in_specs=in_specs,
        out_specs=out_specs,
        out_shape=out_shape,
        compiler_params=_cparams(1),
        name="ffn_mix" if with_mix else ("ffn_proj" if with_proj else "ffn"),
    )(*args)
    return outs if with_proj else outs[0]


def _group_norm_rope(a, gain, cos_t, sin_a, sin_b, scale):
    lane = lax.broadcasted_iota(jnp.int32, (1, LANES), 1)
    first = lane < A_HEAD_DIM
    outs = []
    for p in range(A_W // LANES):
        blk = a[:, p * LANES:(p + 1) * LANES]
        sq = blk * blk
        s0 = jnp.sum(jnp.where(first, sq, 0.0), axis=-1, keepdims=True)
        s1 = jnp.sum(jnp.where(first, 0.0, sq), axis=-1, keepdims=True)
        r = jnp.where(first, lax.rsqrt(s0 * (1.0 / A_HEAD_DIM) + EPS),
                      lax.rsqrt(s1 * (1.0 / A_HEAD_DIM) + EPS))
        y = blk * r * gain[:, p * LANES:(p + 1) * LANES]
        half = ROT_DIM // 2
        y = (y * cos_t + pltpu.roll(y, LANES - half, 1) * sin_a + pltpu.roll(y, half, 1) * sin_b)
        outs.append((y * scale).astype(BF16))
    return jnp.concatenate(outs, axis=-1)


def _mixer_in_proj(x, mod_ref, g_ref, wm_ref, wg_ref, wa_ref, gb_ref, gq_ref, gk_ref,
                   cos_ref, sa_ref, sb_ref,
                   mq_ref, mk_ref, mv_ref, mo_ref, gate_ref, aq_ref, ak_ref, av_ref):
    sh = mod_ref[0, 3:4, :]
    sc = mod_ref[0, 4:5, :]
    h = (_rms(x, g_ref[...]) * (1.0 + sc) + sh).astype(BF16)
    pm = jnp.dot(h, wm_ref[...], preferred_element_type=F32)
    pg = jnp.dot(h, wg_ref[...], preferred_element_type=F32)
    pa = jnp.dot(h, wa_ref[...], preferred_element_type=F32)

    mq_ref[...] = pm[:, :M_QKW].astype(BF16)
    mk_ref[...] = (pm[:, M_QKW:2 * M_QKW] * (M_QK_DIM ** -0.5)).astype(BF16)
    mv_ref[...] = pm[:, 2 * M_QKW:2 * M_QKW + M_W].astype(BF16)
    mo_ref[...] = jax.nn.sigmoid(pm[:, 2 * M_QKW + M_W:]).astype(BF16)
    av_ref[...] = pa[:, 2 * A_W:].astype(BF16)

    gb = pg + gb_ref[...]
    lane = lax.broadcasted_iota(jnp.int32, (1, LANES), 1)
    is_forget = (lane % (2 * M_HEADS)) >= M_HEADS
    log_sig = jnp.minimum(gb, 0.0) - jnp.log(1.0 + jnp.exp(-jnp.abs(gb)))
    gate_ref[...] = jnp.where(is_forget, log_sig, gb).T[:N_GATES, :]

    cos_t, sin_a, sin_b = cos_ref[...], sa_ref[...], sb_ref[...]
    aq_ref[...] = _group_norm_rope(pa[:, :A_W], gq_ref[...], cos_t, sin_a, sin_b,
                                   A_HEAD_DIM ** -0.5 * LOG2E)
    ak_ref[...] = _group_norm_rope(pa[:, A_W:2 * A_W], gk_ref[...], cos_t, sin_a, sin_b, 1.0)


def _rope_tables(seq):
    half = ROT_DIM // 2
    inv_freq = ROPE_THETA ** (-2.0 * np.arange(half) / ROT_DIM)
    ang = np.arange(seq)[:, None] * inv_freq[None, :]
    cos, sin = np.cos(ang), np.sin(ang)
    ones = np.ones((seq, A_HEAD_DIM - ROT_DIM))
    zeros_h = np.zeros((seq, half))
    zeros_r = np.zeros((seq, A_HEAD_DIM - ROT_DIM))
    cos_t = np.concatenate([cos, cos, ones], axis=-1)
    sin_a = np.concatenate([-sin, zeros_h, zeros_r], axis=-1)
    sin_b = np.concatenate([zeros_h, sin, zeros_r], axis=-1)
    return tuple(jnp.asarray(np.tile(t, (1, 2)), F32) for t in (cos_t, sin_a, sin_b))


def _split3(x):
    hi = x.astype(BF16)
    r1 = x - hi.astype(F32)
    mid = r1.astype(BF16)
    lo = (r1 - mid.astype(F32)).astype(BF16)
    return hi, mid, lo


M_TILE = 16
C_ROWS = M_V_DIM + M_TILE
R_ALPHA, R_BETA, R_WI, R_EMT, R_WG, R_MM = range(6)
N_ROW_STATS = 6


def _seg_scan(x, op, fill):
    lane = lax.broadcasted_iota(jnp.int32, (1, LANES), 1)
    first = lane < M_CHUNK
    pos = lane % M_CHUNK
    k = 1
    while k < M_CHUNK:
        from_left = jnp.where(pos >= k, pltpu.roll(x, k, 1), fill)
        from_right = jnp.where(pos < M_CHUNK - k, pltpu.roll(x, LANES - k, 1), fill)
        x = op(x, jnp.where(first, from_left, from_right))
        k *= 2
    return x


def _half_reduce(x, red, fill):
    first = lax.broadcasted_iota(jnp.int32, (1, LANES), 1) < M_CHUNK
    a = red(jnp.where(first, x, fill), axis=-1, keepdims=True)
    b = red(jnp.where(first, fill, x), axis=-1, keepdims=True)
    return jnp.where(first, a, b)


def _mlstm_kernel(q_ref, k_ref, v_ref, og_ref, gi_ref, gf_ref, gmh_ref, out_ref,
                  bt_ref, mg_ref, ms_ref, mn_ref, rows_ref, swap_ref, ut_ref, wgt_ref, cst_ref, cs_ref,
                  *, seq):
    lc = M_CHUNK
    nc = seq // lc
    nrow = M_HEADS * nc
    dn_t = (((0,), (0,)), ((), ()))
    dn_nt = (((1,), (1,)), ((), ()))
    lane = lax.broadcasted_iota(jnp.int32, (1, LANES), 1)
    first = lane < lc

    gi = gi_ref[0]
    gf = gf_ref[0]
    b2 = _seg_scan(gf, jnp.add, 0.0)
    u2 = gi - b2
    cmax2 = _seg_scan(u2, jnp.maximum, NEG_INF)
    btot2 = _half_reduce(gf, jnp.sum, 0.0)
    maxu2 = _half_reduce(u2, jnp.max, NEG_INF)
    mg2 = btot2 + maxu2
    bt_ref[...] = btot2
    mg_ref[...] = mg2

    m = jnp.full((M_HEADS, LANES), NEG_INF, F32)
    for i in range(nc):
        rows_f = pl.ds(i, M_HEADS, stride=nc)
        rows_b = pl.ds(nc - 1 - i, M_HEADS, stride=nc)
        bt = jnp.where(first, bt_ref[rows_f, :], bt_ref[rows_b, :])
        mg = jnp.where(first, mg_ref[rows_f, :], mg_ref[rows_b, :])
        m_new = jnp.maximum(bt + m, mg)
        ms_ref[0, rows_f, :] = m
        ms_ref[1, rows_b, :] = m
        mn_ref[0, rows_f, :] = m_new
        mn_ref[1, rows_b, :] = m_new
        m = m_new
    mstart2 = jnp.where(first, ms_ref[0], ms_ref[1])
    mnext2 = jnp.where(first, mn_ref[0], mn_ref[1])

    mm2 = jnp.maximum(mstart2, cmax2)
    rows_ref[R_ALPHA] = jnp.exp(btot2 + mstart2 - mnext2)
    rows_ref[R_BETA] = jnp.exp(mg2 - mnext2)
    rows_ref[R_WI] = jnp.exp(mstart2 - mm2)
    rows_ref[R_EMT] = jnp.exp(-(b2 + mm2))
    rows_ref[R_WG] = jnp.exp(u2 - maxu2)
    rows_ref[R_MM] = mm2
    for j in range(N_ROW_STATS):
        swap_ref[j] = pltpu.roll(rows_ref[j], lc, 1)

    ut_ref[...] = jnp.zeros(ut_ref.shape, F32)
    wgt_ref[...] = jnp.zeros(wgt_ref.shape, F32)

    def to_tiles(dst_ref, x, row_of):
        by_dir = (x, pltpu.roll(x, lc, 1))
        for h in range(M_HEADS):
            for direction in range(2):
                dst_ref[pl.ds((h // 2) * nc * M_TILE + row_of(2 * direction + h % 2), nc,
                              stride=M_TILE), :] = by_dir[direction][h * nc:(h + 1) * nc]

    for piece, up in enumerate(_split3(u2)):
        to_tiles(ut_ref, up.astype(F32), lambda blk: 3 * blk + piece)
    to_tiles(wgt_ref, rows_ref[R_WG], lambda blk: blk)

    pw = 2 * LANES
    lane2 = lax.broadcasted_iota(jnp.int32, (1, pw), 1)
    blk2 = lane2 // lc
    tile_row = lax.broadcasted_iota(jnp.int32, (M_TILE, pw), 0)
    tile_blk = lax.broadcasted_iota(jnp.int32, (M_TILE, pw), 1) // lc
    spread3 = jnp.where(tile_row // 3 == tile_blk, 1.0, 0.0).astype(BF16)
    spread1 = jnp.where(tile_row == tile_blk, 1.0, 0.0).astype(BF16)

    def pair_row(j, p, c_fwd, c_bwd):
        r0, r1 = (2 * p) * nc, (2 * p + 1) * nc
        return jnp.concatenate(
            [jnp.where(first, rows_ref[j, pl.ds(r0 + c_fwd, 1), :], swap_ref[j, pl.ds(r1 + c_fwd, 1), :]),
             jnp.where(first, swap_ref[j, pl.ds(r0 + c_bwd, 1), :], rows_ref[j, pl.ds(r1 + c_bwd, 1), :])],
            axis=1)

    def twice(ref, rows, p):
        x = ref[0, rows, p * LANES:(p + 1) * LANES]
        return jnp.concatenate([x, x], axis=1)

    cst_ref[...] = jnp.zeros(cst_ref.shape, F32)

    def step(i, carry):
        cf = i
        cb = nc - 1 - i
        rows_f = pl.ds(_aligned(cf * lc, lc), lc)
        rows_b = pl.ds(_aligned(cb * lc, lc), lc)
        for p in range(M_HEADS // 2):
            st = cst_ref[p]
            st_b = st.astype(BF16)
            cs_ref[p, cf, :, 0:LANES] = st_b[:, 0:LANES]
            cs_ref[p, cb, :, LANES:pw] = st_b[:, LANES:pw]
            kw = []
            for rows, c in ((rows_f, cf), (rows_b, cb)):
                tile = pl.ds(_aligned((p * nc + c) * M_TILE, M_TILE), M_TILE)
                wg_rows = lax.dot_general(wgt_ref[tile, 0:lc].astype(BF16), spread1, dn_t,
                                          preferred_element_type=F32)
                kw.append(twice(k_ref, rows, p).astype(F32) * wg_rows)
            n_upd = jnp.where(lane2 < LANES, jnp.sum(kw[0], axis=0, keepdims=True),
                              jnp.sum(kw[1], axis=0, keepdims=True))
            kw = [x.astype(BF16) for x in kw]
            zero = jnp.zeros_like(kw[0])
            k_bd = jnp.concatenate([jnp.where(blk2 == b, kw[b // 2], zero) for b in range(4)], axis=0)
            v_st = jnp.concatenate(
                [v_ref[0, rows, (2 * p + hh) * M_V_DIM:(2 * p + hh + 1) * M_V_DIM]
                 for rows in (rows_f, rows_b) for hh in range(2)], axis=0)
            upd = lax.dot_general(v_st, k_bd, dn_t, preferred_element_type=F32)
            alpha = pair_row(R_ALPHA, p, cf, cb)
            beta = pair_row(R_BETA, p, cf, cb)
            cst_ref[p, 0:M_V_DIM, :] = alpha * st[0:M_V_DIM] + beta * upd
            cst_ref[p, M_V_DIM:M_V_DIM + 1, :] = alpha * st[M_V_DIM:M_V_DIM + 1] + beta * n_upd
        return carry

    lax.fori_loop(0, nc, step, 0, unroll=8)

    s_i = lax.broadcasted_iota(jnp.int32, (lc, pw), 0)
    t_i = lax.broadcasted_iota(jnp.int32, (lc, pw), 1) % lc
    causal = jnp.where((s_i - t_i) * jnp.where(lane2 < LANES, 1, -1) <= 0, 0.0, NEG_INF).astype(F32)

    def chunk(c, carry):
        r0 = _aligned(c * lc, lc)
        rows = pl.ds(r0, lc)
        for p in range(M_HEADS // 2):
            k_d = twice(k_ref, rows, p)
            q_d = twice(q_ref, rows, p)
            zero = jnp.zeros_like(q_d)
            q_bd = jnp.concatenate([jnp.where(blk2 == b, q_d, zero) for b in range(4)], axis=0)
            both = lax.dot_general(jnp.concatenate([k_d, cs_ref[p, c]], axis=0), q_bd, dn_nt,
                                   preferred_element_type=F32)
            qk_t = both[0:lc]
            inter = both[lc:]
            tile = pl.ds(_aligned((p * nc + c) * M_TILE, M_TILE), M_TILE)
            u_rows = lax.dot_general(ut_ref[tile, 0:lc].astype(BF16), spread3, dn_t,
                                     preferred_element_type=F32)
            dm = u_rows - pair_row(R_MM, p, c, c) + causal
            s_t = jnp.exp(dm) * qk_t
            den_in = jnp.sum(s_t, axis=0, keepdims=True)
            s_b = s_t.astype(BF16)
            zero_s = jnp.zeros_like(s_b)
            s_bd = jnp.concatenate([jnp.where(blk2 % 2 == hh, s_b, zero_s) for hh in range(2)], axis=0)
            v_st = jnp.concatenate([v_ref[0, rows, (2 * p + hh) * M_V_DIM:(2 * p + hh + 1) * M_V_DIM]
                                    for hh in range(2)], axis=0)
            num = lax.dot_general(v_st, s_bd, dn_t, preferred_element_type=F32)
            wi = pair_row(R_WI, p, c, c)
            num = num + wi * inter[0:M_V_DIM]
            den = den_in + wi * inter[M_V_DIM:M_V_DIM + 1]
            h_t = num * (1.0 / jnp.maximum(jnp.abs(den), pair_row(R_EMT, p, c, c)))
            hs_t = h_t[:, 0:LANES] + h_t[:, LANES:pw]
            scale = lax.rsqrt(jnp.mean(hs_t * hs_t, axis=0, keepdims=True) + EPS)
            hs = (hs_t * scale).T
            for hh in range(2):
                vs = slice((2 * p + hh) * M_V_DIM, (2 * p + hh + 1) * M_V_DIM)
                y = hs[hh * lc:(hh + 1) * lc] * gmh_ref[:, vs] * og_ref[0, rows, vs].astype(F32)
                out_ref[0, rows, vs] = y.astype(BF16)
        return carry

    lax.fori_loop(0, nc, chunk, 0, unroll=4)


def _mlstm(mqd, mkd, mv, mo, gi2, gf2, gmh, *, seq):
    bsz = mqd.shape[0]
    nc = seq // M_CHUNK
    nrow = M_HEADS * nc
    assert 2 * M_CHUNK == LANES and 2 * M_QK_DIM == LANES and M_V_DIM == LANES
    b3 = lambda b: (b, 0, 0)
    slab = pltpu.VMEM((nrow, LANES), F32)
    return pl.pallas_call(
        functools.partial(_mlstm_kernel, seq=seq),
        grid=(bsz,),
        in_specs=[pl.BlockSpec((1, seq, M_QKW), b3),
                  pl.BlockSpec((1, seq, M_QKW), b3),
                  pl.BlockSpec((1, seq, M_W), b3),
                  pl.BlockSpec((1, seq, M_W), b3),
                  pl.BlockSpec((1, nrow, LANES), b3),
                  pl.BlockSpec((1, nrow, LANES), b3),
                  pl.BlockSpec((1, M_W), lambda b: (0, 0))],
        out_specs=pl.BlockSpec((1, seq, M_W), b3),
        out_shape=jax.ShapeDtypeStruct((bsz, seq, M_W), BF16),
        scratch_shapes=[slab, slab,
                        pltpu.VMEM((2, nrow, LANES), F32),
                        pltpu.VMEM((2, nrow, LANES), F32),
                        pltpu.VMEM((N_ROW_STATS, nrow, LANES), F32),
                        pltpu.VMEM((N_ROW_STATS, nrow, LANES), F32),
                        pltpu.VMEM((M_HEADS // 2 * nc * M_TILE, LANES), F32),
                        pltpu.VMEM((M_HEADS // 2 * nc * M_TILE, LANES), F32),
                        pltpu.VMEM((M_HEADS // 2, C_ROWS, 2 * LANES), F32),
                        pltpu.VMEM((M_HEADS // 2, nc, C_ROWS, 2 * LANES), BF16)],
        compiler_params=_cparams(1),
        name="mlstm",
    )(mqd, mkd, mv, mo, gi2, gf2, gmh)


ATT_QB = 128
ATT_SIDE = 64
ATT_KW = ATT_QB + 2 * ATT_SIDE
LOG2E = 1.4426950408889634
assert all(w // (2 * d) == ATT_SIDE for w, d in zip(WINDOWS, DILATIONS))
assert DILATIONS == (1, 4, 16)


def _band_bias(kw, offset):
    qi = lax.broadcasted_iota(jnp.int32, (2 * ATT_QB, kw), 0) % ATT_QB
    ki = lax.broadcasted_iota(jnp.int32, (2 * ATT_QB, kw), 1)
    return jnp.where(jnp.abs(ki + offset - qi) <= ATT_SIDE, 0.0, NEG_INF).astype(F32)


def _attn_block(qb, kb, vb, bias):
    first = lax.broadcasted_iota(jnp.int32, (1, LANES), 1) < A_HEAD_DIM
    zero = jnp.zeros_like(qb)
    qm = jnp.concatenate([jnp.where(first, qb, zero), jnp.where(first, zero, qb)], axis=0)
    s = lax.dot_general(qm, kb, (((1,), (1,)), ((), ())), preferred_element_type=F32) + bias
    m = jnp.max(s, axis=-1, keepdims=True)
    p = jnp.exp2(s - m)
    den = jnp.sum(p, axis=-1, keepdims=True)
    o = jnp.dot(p.astype(BF16), vb, preferred_element_type=F32) / den
    lse = m + jnp.log2(den)
    return (jnp.where(first, o[:ATT_QB], o[ATT_QB:]),
            jnp.where(first, lse[:ATT_QB], lse[ATT_QB:]))


def _merge(o_a, l_a, o_b, l_b):
    top = jnp.maximum(l_a, l_b)
    w_a = jnp.exp2(l_a - top)
    w_b = jnp.exp2(l_b - top)
    tot = w_a + w_b
    return (w_a * o_a + w_b * o_b) / tot, top + jnp.log2(tot)


def _dil_attn_kernel(q_ref, k_ref, v_ref, out_ref, tok_ref, r4f_ref, r4_ref, r16_ref,
                     o4_ref, l4_ref, ot_ref, lt_ref, bias_ref, *, seq):
    l4 = seq // 4
    l16 = seq // 16
    n_in = 3

    @pl.when((pl.program_id(0) == 0) & (pl.program_id(1) == 0))
    def _():
        bias_ref[0] = _band_bias(ATT_KW, 0)
        bias_ref[1] = _band_bias(ATT_KW, -ATT_SIDE)
        bias_ref[2] = _band_bias(ATT_KW, ATT_QB - ATT_KW)
        bias_ref[3, :, :ATT_QB] = _band_bias(ATT_QB, 0)

    for i, ref in enumerate((q_ref, k_ref, v_ref)):
        tok_ref[i] = ref[0].astype(F32)
    for i in range(n_in):
        for r in range(4):
            x = tok_ref[i, pl.ds(r, l4, stride=4), :]
            r4f_ref[i, r * l4:(r + 1) * l4, :] = x
            r4_ref[i, r * l4:(r + 1) * l4, :] = x.astype(BF16)
    for i in range(n_in):
        for c in range(16):
            x = r4f_ref[i, pl.ds((c // 4) * l4 + c % 4, l16, stride=4), :]
            r16_ref[i, c * l16:(c + 1) * l16, :] = x.astype(BF16)

    bias3 = bias_ref[3, :, :ATT_QB]

    def pat3(r4, carry):
        for r2 in range(4):
            q0 = _aligned((r4 * 4 + r2) * l16, l16)
            rows = pl.ds(q0, l16)
            o, l = _attn_block(r16_ref[0, rows, :], r16_ref[1, rows, :], r16_ref[2, rows, :], bias3)
            dst = pl.ds(r4 * l4 + r2, l16, stride=4)
            o4_ref[dst, :] = o
            l4_ref[dst, :] = l
        return carry

    lax.fori_loop(0, 4, pat3, 0, unroll=4)

    def window(ref, i, q0, jb, nb):
        off = 0 if jb == 0 else (ATT_QB - ATT_KW if jb == nb - 1 else -ATT_SIDE)
        return ref[i, pl.ds(_aligned(q0 + off, ATT_SIDE), ATT_KW), :]

    def bias_of(jb, nb):
        return bias_ref[0 if jb == 0 else (2 if jb == nb - 1 else 1)]

    nb4 = l4 // ATT_QB

    def pat2(r, carry):
        for jb in range(nb4):
            q0 = _aligned(r * l4 + jb * ATT_QB, ATT_QB)
            rows = pl.ds(q0, ATT_QB)
            o, l = _attn_block(r4_ref[0, rows, :], window(r4_ref, 1, q0, jb, nb4),
                               window(r4_ref, 2, q0, jb, nb4), bias_of(jb, nb4))
            o, l = _merge(o, l, o4_ref[rows, :], l4_ref[rows, :])
            dst = pl.ds(4 * jb * ATT_QB + r, ATT_QB, stride=4)
            ot_ref[dst, :] = o
            lt_ref[dst, :] = l
        return carry

    lax.fori_loop(0, 4, pat2, 0, unroll=4)

    nb1 = seq // ATT_QB

    def pat1_block(n, jb):
        q0 = _aligned(n * ATT_QB, ATT_QB)
        rows = pl.ds(q0, ATT_QB)
        o, l = _attn_block(q_ref[0, rows, :], window(k_ref, 0, q0, jb, nb1),
                           window(v_ref, 0, q0, jb, nb1), bias_of(jb, nb1))
        o, _ = _merge(o, l, ot_ref[rows, :], lt_ref[rows, :])
        out_ref[0, rows, :] = o.astype(BF16)

    per_iter = (nb1 - 2) // 2

    def pat1(g, carry):
        for u in range(per_iter):
            pat1_block(per_iter * g + 1 + u, 1)
        return carry

    pat1_block(0, 0)
    lax.fori_loop(0, 2, pat1, 0, unroll=2)
    for n in range(2 * per_iter + 1, nb1):
        pat1_block(n, n)


def _dil_attn(aq, ak, av, *, seq):
    bsz = aq.shape[0]
    n_pairs = A_W // LANES
    blk = pl.BlockSpec((1, seq, LANES), lambda b, p: (b, 0, p))
    f32_slab = pltpu.VMEM((seq, LANES), F32)
    return pl.pallas_call(
        functools.partial(_dil_attn_kernel, seq=seq),
        grid=(bsz, n_pairs),
        in_specs=[blk, blk, blk],
        out_specs=blk,
        out_shape=jax.ShapeDtypeStruct((bsz, seq, A_W), BF16),
        scratch_shapes=[pltpu.VMEM((3, seq, LANES), F32),
                        pltpu.VMEM((3, seq, LANES), F32),
                        pltpu.VMEM((3, seq, LANES), BF16),
                        pltpu.VMEM((3, seq, LANES), BF16),
                        f32_slab, f32_slab, f32_slab, f32_slab,
                        pltpu.VMEM((4, 2 * ATT_QB, ATT_KW), F32)],
        compiler_params=_cparams(2),
        name="dil_attn",
    )(aq, ak, av)


def _layer(x, mod3, g_ffn1, w_gu1, w_down1, g_mix, w_in, gate_bias, g_q, g_k, g_mh, w_out,
           g_ffn2, w_gu2, w_down2, g_final):
    bsz, seq, d = x.shape
    t = bsz * seq
    tm = 512
    row2 = lambda v: v.reshape(1, -1).astype(F32)

    o_aq = M_PROJ_W + N_GATES
    w_m = w_in[:, :M_PROJ_W].astype(BF16)
    w_g = jnp.pad(w_in[:, M_PROJ_W:o_aq], ((0, 0), (0, LANES - N_GATES))).astype(BF16)
    w_a = w_in[:, o_aq:].astype(BF16)
    gb = jnp.concatenate([gate_bias.reshape(1, N_GATES).astype(F32),
                          jnp.zeros((1, LANES - N_GATES), F32)], axis=1)
    gq = jnp.tile(row2(g_q), (1, A_HEADS))
    gk = jnp.tile(row2(g_k), (1, A_HEADS))
    x1, mq, mk, mv, mo, gates, aq, ak, av = _ffn(
        x.reshape(t, d), mod3, row2(g_ffn1), w_gu1.astype(BF16), w_down1.astype(BF16),
        seq=seq, mod_base=0, tm=tm,
        proj=(row2(g_mix), w_m, w_g, w_a, gb, gq, gk, _rope_tables(seq)))

    to3 = lambda a: a.reshape(bsz, seq, a.shape[-1])
    nc = seq // M_CHUNK
    g5 = gates.reshape(2, 2, M_HEADS, bsz, nc, M_CHUNK)
    g5 = g5.transpose(1, 3, 2, 4, 0, 5).reshape(2, bsz, M_HEADS * nc, 2 * M_CHUNK)
    hm = _mlstm(to3(mq), to3(mk), to3(mv), to3(mo), g5[0], g5[1], row2(g_mh), seq=seq)
    ha = _dil_attn(to3(aq), to3(ak), to3(av), seq=seq)

    w_out_b = w_out.astype(BF16)
    out = _ffn(x1, mod3, row2(g_ffn2), w_gu2.astype(BF16), w_down2.astype(BF16),
               seq=seq, mod_base=6, tm=tm,
               mix=(hm.reshape(t, M_W), ha.reshape(t, A_W), w_out_b[:M_W], w_out_b[M_W:]),
               gfin=row2(g_final))
    return out.reshape(bsz, seq, d)


def kernel(x, c, w_ada, b_ada, g_ffn1, w_gu1, w_down1, g_mix, w_in, gate_bias, g_q, g_k, g_mh,
           w_out, g_ffn2, w_gu2, w_down2, g_final):
    bsz, seq, d = x.shape
    depth = w_ada.shape[0]
    for l in range(depth):
        mod = _adaln(c, w_ada[l], b_ada[l].reshape(1, -1))
        mod3 = mod.reshape(bsz, N_MOD, d)
        x = _layer(x, mod3, g_ffn1[l], w_gu1[l], w_down1[l], g_mix[l], w_in[l], gate_bias[l],
                   g_q[l], g_k[l], g_mh[l], w_out[l], g_ffn2[l], w_gu2[l], w_down2[l], g_final[l])
    return x
```

```python
import functools

import jax
import jax.numpy as jnp
import numpy as np
from jax import lax
from jax.experimental import pallas as pl
from jax.experimental.pallas import tpu as pltpu

F32 = jnp.float32
BF16 = jnp.bfloat16

D_MODEL = 1024
D_FF = 2816
N_MOD = 9
M_HEADS = 4
M_V_DIM = 128
M_QK_DIM = 64
M_CHUNK = 64
A_HEADS = 8
A_HEAD_DIM = 64
WINDOWS = (128, 512, 2048)
DILATIONS = (1, 4, 16)
ROT_DIM = A_HEAD_DIM // 4
ROPE_THETA = 500000.0
EPS = 1e-6
NEG_INF = -1e30
HALF_STEP = 0.5

LANES = 128
M_W = M_HEADS * M_V_DIM
M_QKW = M_HEADS * M_QK_DIM
A_W = A_HEADS * A_HEAD_DIM
N_GATES = 4 * M_HEADS
M_PROJ_W = 2 * M_QKW + 2 * M_W
A_PROJ_W = 3 * A_W

VMEM_LIMIT = 56 * 1024 * 1024


def _cparams(n_axes):
    return pltpu.CompilerParams(dimension_semantics=("arbitrary",) * n_axes,
                                vmem_limit_bytes=VMEM_LIMIT)


def _aligned(x, m):
    return x if isinstance(x, int) else pl.multiple_of(x, m)


def _rms(xf, g):
    ms = jnp.mean(xf * xf, axis=-1, keepdims=True)
    return xf * lax.rsqrt(ms + EPS) * g


def _adaln_kernel(c_ref, w_ref, b_ref, o_ref):
    c = c_ref[...]
    cs = (c * jax.nn.sigmoid(c)).astype(BF16)
    o_ref[...] = jnp.dot(cs, w_ref[...].astype(BF16), preferred_element_type=F32) + b_ref[...]


def _adaln(c, w, b):
    bsz, d = c.shape
    n = w.shape[1]
    tn = 1024
    return pl.pallas_call(
        _adaln_kernel,
        grid=(n // tn,),
        in_specs=[pl.BlockSpec((bsz, d), lambda j: (0, 0)),
                  pl.BlockSpec((d, tn), lambda j: (0, j)),
                  pl.BlockSpec((1, tn), lambda j: (0, j))],
        out_specs=pl.BlockSpec((bsz, tn), lambda j: (0, j)),
        out_shape=jax.ShapeDtypeStruct((bsz, n), F32),
        compiler_params=_cparams(1),
        name="adaln",
    )(c, w, b)


FFN_SUB_TILES = 2
N_PROJ_IN = 11
N_PROJ_OUT = 8
PROJ_GATE_OUT = 4


W_CHUNKS = 16


def _load_cast(w_hbm, dst_ref, stage_ref, sem_ref):
    n_rows = w_hbm.shape[0]
    rows = n_rows // W_CHUNKS
    assert rows * W_CHUNKS == n_rows and rows % 8 == 0 and rows <= stage_ref.shape[1]

    def copy(c):
        return pltpu.make_async_copy(w_hbm.at[pl.ds(c * rows, rows), :],
                                     stage_ref.at[c % 2, pl.ds(0, rows), :], sem_ref.at[c % 2])

    copy(0).start()
    for c in range(W_CHUNKS):
        if c + 1 < W_CHUNKS:
            copy(c + 1).start()
        copy(c).wait()
        dst_ref[pl.ds(c * rows, rows), :] = stage_ref[c % 2, 0:rows, :].astype(BF16)


def _ffn_kernel(*refs, mod_base, with_mix, final_norm, with_proj):
    it = iter(refs)
    x_ref = next(it)
    if with_mix:
        hm_ref, ha_ref, wo_hbm = next(it), next(it), next(it)
    mod_ref, g_ref, wgu_hbm, wd_hbm = next(it), next(it), next(it), next(it)
    if final_norm:
        gfin_ref = next(it)
    if with_proj:
        proj_in = [next(it) for _ in range(N_PROJ_IN)]
    o_ref = next(it)
    if with_proj:
        proj_out = [next(it) for _ in range(N_PROJ_OUT)]
        xprev_ref = next(it)
    wgu_ref, wd_ref, stage_gu_ref, stage_d_ref, sem_ref = next(it), next(it), next(it), next(it), next(it)
    if with_mix:
        wo_ref = next(it)
        wo_m_ref, wo_a_ref = wo_ref.at[pl.ds(0, M_W)], wo_ref.at[pl.ds(M_W, A_W)]

    @pl.when(pl.program_id(0) == 0)
    def _():
        _load_cast(wgu_hbm, wgu_ref, stage_gu_ref, sem_ref)
        _load_cast(wd_hbm, wd_ref, stage_d_ref, sem_ref)
        if with_mix:
            _load_cast(wo_hbm, wo_ref, stage_d_ref, sem_ref)

    sub = x_ref.shape[0] // FFN_SUB_TILES
    if with_proj:
        @pl.when(pl.program_id(0) == 0)
        def _():
            xprev_ref[...] = jnp.zeros(xprev_ref.shape, F32)

        for s in range(FFN_SUB_TILES):
            rows = pl.ds(s * sub, sub)
            tables = [r.at[rows] for r in proj_in[N_PROJ_IN - 3:]]
            outs = [r.at[:, rows] if i == PROJ_GATE_OUT else r.at[rows] for i, r in enumerate(proj_out)]
            _mixer_in_proj(xprev_ref[rows, :], *proj_in[:N_PROJ_IN - 3], *tables, *outs)

    sh = mod_ref[0, mod_base:mod_base + 1, :]
    sc = mod_ref[0, mod_base + 1:mod_base + 2, :]
    gt = mod_ref[0, mod_base + 2:mod_base + 3, :]
    for s in range(FFN_SUB_TILES):
        rows = pl.ds(s * sub, sub)
        x = x_ref[rows, :]
        if with_mix:
            gt_mix = mod_ref[0, mod_base - 1:mod_base, :]
            y = (jnp.dot(hm_ref[rows, :], wo_m_ref[...], preferred_element_type=F32)
                 + jnp.dot(ha_ref[rows, :], wo_a_ref[...], preferred_element_type=F32))
            x = x + gt_mix * y
        h = (_rms(x, g_ref[...]) * (1.0 + sc) + sh).astype(BF16)
        gu = jnp.dot(h, wgu_ref[...], preferred_element_type=F32)
        g = gu[:, :D_FF]
        u = gu[:, D_FF:]
        a = (g * jax.nn.sigmoid(g) * u).astype(BF16)
        y = jnp.dot(a, wd_ref[...], preferred_element_type=F32)
        x = x + HALF_STEP * gt * y
        if final_norm:
            x = _rms(x, gfin_ref[...])
        o_ref[rows, :] = x
        if with_proj:
            xprev_ref[rows, :] = x


def _ffn(x2d, mod3, g, wgu, wd, *, seq, mod_base, tm, mix=None, gfin=None, proj=None):
    t, d = x2d.shape
    with_mix = mix is not None
    final_norm = gfin is not None
    with_proj = proj is not None
    n_tiles = t // tm
    cur = (lambda i: jnp.minimum(i, n_tiles - 1)) if with_proj else (lambda i: i)
    prev = lambda i: jnp.maximum(i - 1, 0)
    row = lambda i: (cur(i), 0)
    const = lambda i: (0, 0)
    resident = pl.Buffered(1)
    args = [x2d]
    in_specs = [pl.BlockSpec((tm, d), row)]
    in_hbm = pl.BlockSpec(memory_space=pl.ANY)
    if with_mix:
        hm, ha, w_out = mix
        args += [hm, ha, w_out]
        in_specs += [pl.BlockSpec((tm, M_W), row), pl.BlockSpec((tm, A_W), row), in_hbm]
    args += [mod3, g, wgu, wd]
    in_specs += [pl.BlockSpec((1, N_MOD, d), lambda i: ((cur(i) * tm) // seq, 0, 0)),
                 pl.BlockSpec((1, d), const), in_hbm, in_hbm]
    if final_norm:
        args.append(gfin)
        in_specs.append(pl.BlockSpec((1, d), const))
    out_shape = [jax.ShapeDtypeStruct((t, d), F32)]
    if with_proj:
        g_mix, w_m, w_g, w_a, gate_bias, gq, gk, tables = proj
        pos = lambda i: ((prev(i) * tm) % seq // tm, 0)
        args += [mod3, g_mix, w_m, w_g, w_a, gate_bias, gq, gk, *tables]
        in_specs += [pl.BlockSpec((1, N_MOD, d), lambda i: ((prev(i) * tm) // seq, 0, 0)),
                     pl.BlockSpec((1, d), const),
                     pl.BlockSpec((d, M_PROJ_W), const, pipeline_mode=resident),
                     pl.BlockSpec((d, LANES), const, pipeline_mode=resident),
                     pl.BlockSpec((d, A_PROJ_W), const, pipeline_mode=resident),
                     pl.BlockSpec((1, LANES), const),
                     pl.BlockSpec((1, A_W), const),
                     pl.BlockSpec((1, A_W), const),
                     pl.BlockSpec((tm, LANES), pos),
                     pl.BlockSpec((tm, LANES), pos),
                     pl.BlockSpec((tm, LANES), pos)]
        bf = lambda n: jax.ShapeDtypeStruct((t, n), BF16)
        out_shape += [bf(M_QKW), bf(M_QKW), bf(M_W), bf(M_W),
                      jax.ShapeDtypeStruct((N_GATES, t), F32), bf(A_W), bf(A_W), bf(A_W)]
    out_specs = [pl.BlockSpec((tm, d), row)]
    if with_proj:
        out_specs += [pl.BlockSpec((tm, s.shape[1]), lambda i: (prev(i), 0)) for s in out_shape[1:]]
        out_specs[1 + PROJ_GATE_OUT] = pl.BlockSpec((N_GATES, tm), lambda i: (0, prev(i)))
    outs = pl.pallas_call(
        functools.partial(_ffn_kernel, mod_base=mod_base, with_mix=with_mix, final_norm=final_norm,
                          with_proj=with_proj),
        grid=(n_tiles + 1 if with_proj else n_tiles,),
        in_specs=in_specs,
        out_specs=out_specs,
        out_shape=out_shape,
        scratch_shapes=(([pltpu.VMEM((tm, d), F32)] if with_proj else [])
                        + [pltpu.VMEM((d, 2 * D_FF), BF16), pltpu.VMEM((D_FF, d), BF16),
                           pltpu.VMEM((2, d // W_CHUNKS, 2 * D_FF), F32),
                           pltpu.VMEM((2, D_FF // W_CHUNKS, d), F32),
                           pltpu.SemaphoreType.DMA((2,))]
                        + ([pltpu.VMEM((M_W + A_W, d), BF16)] if with_mix else [])),
        compiler_params=_cparams(1),
        name="ffn_mix" if with_mix else ("ffn_proj" if with_proj else "ffn"),
    )(*args)
    return outs if with_proj else outs[0]


def _group_norm_rope(a, gain, cos_t, sin_a, sin_b, scale):
    lane = lax.broadcasted_iota(jnp.int32, (1, LANES), 1)
    first = lane < A_HEAD_DIM
    outs = []
    for p in range(A_W // LANES):
        blk = a[:, p * LANES:(p + 1) * LANES]
        sq = blk * blk
        s0 = jnp.sum(jnp.where(first, sq, 0.0), axis=-1, keepdims=True)
        s1 = jnp.sum(jnp.where(first, 0.0, sq), axis=-1, keepdims=True)
        r = jnp.where(first, lax.rsqrt(s0 * (1.0 / A_HEAD_DIM) + EPS),
                      lax.rsqrt(s1 * (1.0 / A_HEAD_DIM) + EPS))
        y = blk * r * gain[:, p * LANES:(p + 1) * LANES]
        half = ROT_DIM // 2
        y = (y * cos_t + pltpu.roll(y, LANES - half, 1) * sin_a + pltpu.roll(y, half, 1) * sin_b)
        outs.append((y * scale).astype(BF16))
    return jnp.concatenate(outs, axis=-1)


def _mixer_in_proj(x, mod_ref, g_ref, wm_ref, wg_ref, wa_ref, gb_ref, gq_ref, gk_ref,
                   cos_ref, sa_ref, sb_ref,
                   mq_ref, mk_ref, mv_ref, mo_ref, gate_ref, aq_ref, ak_ref, av_ref):
    sh = mod_ref[0, 3:4, :]
    sc = mod_ref[0, 4:5, :]
    h = (_rms(x, g_ref[...]) * (1.0 + sc) + sh).astype(BF16)
    pm = jnp.dot(h, wm_ref[...], preferred_element_type=F32)
    pg = jnp.dot(h, wg_ref[...], preferred_element_type=F32)
    pa = jnp.dot(h, wa_ref[...], preferred_element_type=F32)

    mq_ref[...] = pm[:, :M_QKW].astype(BF16)
    mk_ref[...] = (pm[:, M_QKW:2 * M_QKW] * (M_QK_DIM ** -0.5)).astype(BF16)
    mv_ref[...] = pm[:, 2 * M_QKW:2 * M_QKW + M_W].astype(BF16)
    mo_ref[...] = jax.nn.sigmoid(pm[:, 2 * M_QKW + M_W:]).astype(BF16)
    av_ref[...] = pa[:, 2 * A_W:].astype(BF16)

    gb = pg + gb_ref[...]
    lane = lax.broadcasted_iota(jnp.int32, (1, LANES), 1)
    is_forget = (lane % (2 * M_HEADS)) >= M_HEADS
    log_sig = jnp.minimum(gb, 0.0) - jnp.log(1.0 + jnp.exp(-jnp.abs(gb)))
    gate_ref[...] = jnp.where(is_forget, log_sig, gb).T[:N_GATES, :]

    cos_t, sin_a, sin_b = cos_ref[...], sa_ref[...], sb_ref[...]
    aq_ref[...] = _group_norm_rope(pa[:, :A_W], gq_ref[...], cos_t, sin_a, sin_b,
                                   A_HEAD_DIM ** -0.5 * LOG2E)
    ak_ref[...] = _group_norm_rope(pa[:, A_W:2 * A_W], gk_ref[...], cos_t, sin_a, sin_b, 1.0)


def _rope_tables(seq):
    half = ROT_DIM // 2
    inv_freq = ROPE_THETA ** (-2.0 * np.arange(half) / ROT_DIM)
    ang = np.arange(seq)[:, None] * inv_freq[None, :]
    cos, sin = np.cos(ang), np.sin(ang)
    ones = np.ones((seq, A_HEAD_DIM - ROT_DIM))
    zeros_h = np.zeros((seq, half))
    zeros_r = np.zeros((seq, A_HEAD_DIM - ROT_DIM))
    cos_t = np.concatenate([cos, cos, ones], axis=-1)
    sin_a = np.concatenate([-sin, zeros_h, zeros_r], axis=-1)
    sin_b = np.concatenate([zeros_h, sin, zeros_r], axis=-1)
    return tuple(jnp.asarray(np.tile(t, (1, 2)), F32) for t in (cos_t, sin_a, sin_b))


def _split3(x):
    hi = x.astype(BF16)
    r1 = x - hi.astype(F32)
    mid = r1.astype(BF16)
    lo = (r1 - mid.astype(F32)).astype(BF16)
    return hi, mid, lo


M_TILE = 16
C_ROWS = M_V_DIM + M_TILE
R_ALPHA, R_BETA, R_WI, R_EMT, R_WG, R_MM = range(6)
N_ROW_STATS = 6


def _seg_scan(x, op, fill):
    lane = lax.broadcasted_iota(jnp.int32, (1, LANES), 1)
    first = lane < M_CHUNK
    pos = lane % M_CHUNK
    k = 1
    while k < M_CHUNK:
        from_left = jnp.where(pos >= k, pltpu.roll(x, k, 1), fill)
        from_right = jnp.where(pos < M_CHUNK - k, pltpu.roll(x, LANES - k, 1), fill)
        x = op(x, jnp.where(first, from_left, from_right))
        k *= 2
    return x


def _half_reduce(x, red, fill):
    first = lax.broadcasted_iota(jnp.int32, (1, LANES), 1) < M_CHUNK
    a = red(jnp.where(first, x, fill), axis=-1, keepdims=True)
    b = red(jnp.where(first, fill, x), axis=-1, keepdims=True)
    return jnp.where(first, a, b)


def _mlstm_kernel(q_ref, k_ref, v_ref, og_ref, gi_ref, gf_ref, gmh_ref, out_ref,
                  bt_ref, mg_ref, ms_ref, mn_ref, rows_ref, swap_ref, ut_ref, wgt_ref, cst_ref, cs_ref,
                  *, seq):
    lc = M_CHUNK
    nc = seq // lc
    nrow = M_HEADS * nc
    dn_t = (((0,), (0,)), ((), ()))
    dn_nt = (((1,), (1,)), ((), ()))
    lane = lax.broadcasted_iota(jnp.int32, (1, LANES), 1)
    first = lane < lc

    gi = gi_ref[0]
    gf = gf_ref[0]
    b2 = _seg_scan(gf, jnp.add, 0.0)
    u2 = gi - b2
    cmax2 = _seg_scan(u2, jnp.maximum, NEG_INF)
    btot2 = _half_reduce(gf, jnp.sum, 0.0)
    maxu2 = _half_reduce(u2, jnp.max, NEG_INF)
    mg2 = btot2 + maxu2
    bt_ref[...] = btot2
    mg_ref[...] = mg2

    m = jnp.full((M_HEADS, LANES), NEG_INF, F32)
    for i in range(nc):
        rows_f = pl.ds(i, M_HEADS, stride=nc)
        rows_b = pl.ds(nc - 1 - i, M_HEADS, stride=nc)
        bt = jnp.where(first, bt_ref[rows_f, :], bt_ref[rows_b, :])
        mg = jnp.where(first, mg_ref[rows_f, :], mg_ref[rows_b, :])
        m_new = jnp.maximum(bt + m, mg)
        ms_ref[0, rows_f, :] = m
        ms_ref[1, rows_b, :] = m
        mn_ref[0, rows_f, :] = m_new
        mn_ref[1, rows_b, :] = m_new
        m = m_new
    mstart2 = jnp.where(first, ms_ref[0], ms_ref[1])
    mnext2 = jnp.where(first, mn_ref[0], mn_ref[1])

    mm2 = jnp.maximum(mstart2, cmax2)
    rows_ref[R_ALPHA] = jnp.exp(btot2 + mstart2 - mnext2)
    rows_ref[R_BETA] = jnp.exp(mg2 - mnext2)
    rows_ref[R_WI] = jnp.exp(mstart2 - mm2)
    rows_ref[R_EMT] = jnp.exp(-(b2 + mm2))
    rows_ref[R_WG] = jnp.exp(u2 - maxu2)
    rows_ref[R_MM] = mm2
    for j in range(N_ROW_STATS):
        swap_ref[j] = pltpu.roll(rows_ref[j], lc, 1)

    ut_ref[...] = jnp.zeros(ut_ref.shape, F32)
    wgt_ref[...] = jnp.zeros(wgt_ref.shape, F32)

    def to_tiles(dst_ref, x, row_of):
        by_dir = (x, pltpu.roll(x, lc, 1))
        for h in range(M_HEADS):
            for direction in range(2):
                dst_ref[pl.ds((h // 2) * nc * M_TILE + row_of(2 * direction + h % 2), nc,
                              stride=M_TILE), :] = by_dir[direction][h * nc:(h + 1) * nc]

    for piece, up in enumerate(_split3(u2)):
        to_tiles(ut_ref, up.astype(F32), lambda blk: 3 * blk + piece)
    to_tiles(wgt_ref, rows_ref[R_WG], lambda blk: blk)

    pw = 2 * LANES
    lane2 = lax.broadcasted_iota(jnp.int32, (1, pw), 1)
    blk2 = lane2 // lc
    tile_row = lax.broadcasted_iota(jnp.int32, (M_TILE, pw), 0)
    tile_blk = lax.broadcasted_iota(jnp.int32, (M_TILE, pw), 1) // lc
    spread3 = jnp.where(tile_row // 3 == tile_blk, 1.0, 0.0).astype(BF16)
    spread1 = jnp.where(tile_row == tile_blk, 1.0, 0.0).astype(BF16)

    def pair_row(j, p, c_fwd, c_bwd):
        r0, r1 = (2 * p) * nc, (2 * p + 1) * nc
        return jnp.concatenate(
            [jnp.where(first, rows_ref[j, pl.ds(r0 + c_fwd, 1), :], swap_ref[j, pl.ds(r1 + c_fwd, 1), :]),
             jnp.where(first, swap_ref[j, pl.ds(r0 + c_bwd, 1), :], rows_ref[j, pl.ds(r1 + c_bwd, 1), :])],
            axis=1)

    def twice(ref, rows, p):
        x = ref[0, rows, p * LANES:(p + 1) * LANES]
        return jnp.concatenate([x, x], axis=1)

    cst_ref[...] = jnp.zeros(cst_ref.shape, F32)

    def step(i, carry):
        cf = i
        cb = nc - 1 - i
        rows_f = pl.ds(_aligned(cf * lc, lc), lc)
        rows_b = pl.ds(_aligned(cb * lc, lc), lc)
        for p in range(M_HEADS // 2):
            st = cst_ref[p]
            st_b = st.astype(BF16)
            cs_ref[p, cf, :, 0:LANES] = st_b[:, 0:LANES]
            cs_ref[p, cb, :, LANES:pw] = st_b[:, LANES:pw]
            kw = []
            for rows, c in ((rows_f, cf), (rows_b, cb)):
                tile = pl.ds(_aligned((p * nc + c) * M_TILE, M_TILE), M_TILE)
                wg_rows = lax.dot_general(wgt_ref[tile, 0:lc].astype(BF16), spread1, dn_t,
                                          preferred_element_type=F32)
                kw.append(twice(k_ref, rows, p).astype(F32) * wg_rows)
            n_upd = jnp.where(lane2 < LANES, jnp.sum(kw[0], axis=0, keepdims=True),
                              jnp.sum(kw[1], axis=0, keepdims=True))
            kw = [x.astype(BF16) for x in kw]
            zero = jnp.zeros_like(kw[0])
            k_bd = jnp.concatenate([jnp.where(blk2 == b, kw[b // 2], zero) for b in range(4)], axis=0)
            v_st = jnp.concatenate(
                [v_ref[0, rows, (2 * p + hh) * M_V_DIM:(2 * p + hh + 1) * M_V_DIM]
                 for rows in (rows_f, rows_b) for hh in range(2)], axis=0)
            upd = lax.dot_general(v_st, k_bd, dn_t, preferred_element_type=F32)
            alpha = pair_row(R_ALPHA, p, cf, cb)
            beta = pair_row(R_BETA, p, cf, cb)
            cst_ref[p, 0:M_V_DIM, :] = alpha * st[0:M_V_DIM] + beta * upd
            cst_ref[p, M_V_DIM:M_V_DIM + 1, :] = alpha * st[M_V_DIM:M_V_DIM + 1] + beta * n_upd
        return carry

    lax.fori_loop(0, nc, step, 0, unroll=8)

    s_i = lax.broadcasted_iota(jnp.int32, (lc, pw), 0)
    t_i = lax.broadcasted_iota(jnp.int32, (lc, pw), 1) % lc
    causal = jnp.where((s_i - t_i) * jnp.where(lane2 < LANES, 1, -1) <= 0, 0.0, NEG_INF).astype(F32)

    def chunk(c, carry):
        r0 = _aligned(c * lc, lc)
        rows = pl.ds(r0, lc)
        for p in range(M_HEADS // 2):
            k_d = twice(k_ref, rows, p)
            q_d = twice(q_ref, rows, p)
            zero = jnp.zeros_like(q_d)
            q_bd = jnp.concatenate([jnp.where(blk2 == b, q_d, zero) for b in range(4)], axis=0)
            both = lax.dot_general(jnp.concatenate([k_d, cs_ref[p, c]], axis=0), q_bd, dn_nt,
                                   preferred_element_type=F32)
            qk_t = both[0:lc]
            inter = both[lc:]
            tile = pl.ds(_aligned((p * nc + c) * M_TILE, M_TILE), M_TILE)
            u_rows = lax.dot_general(ut_ref[tile, 0:lc].astype(BF16), spread3, dn_t,
                                     preferred_element_type=F32)
            dm = u_rows - pair_row(R_MM, p, c, c) + causal
            s_t = jnp.exp(dm) * qk_t
            den_in = jnp.sum(s_t, axis=0, keepdims=True)
            s_b = s_t.astype(BF16)
            zero_s = jnp.zeros_like(s_b)
            s_bd = jnp.concatenate([jnp.where(blk2 % 2 == hh, s_b, zero_s) for hh in range(2)], axis=0)
            v_st = jnp.concatenate([v_ref[0, rows, (2 * p + hh) * M_V_DIM:(2 * p + hh + 1) * M_V_DIM]
                                    for hh in range(2)], axis=0)
            num = lax.dot_general(v_st, s_bd, dn_t, preferred_element_type=F32)
            wi = pair_row(R_WI, p, c, c)
            num = num + wi * inter[0:M_V_DIM]
            den = den_in + wi * inter[M_V_DIM:M_V_DIM + 1]
            h_t = num * (1.0 / jnp.maximum(jnp.abs(den), pair_row(R_EMT, p, c, c)))
            hs_t = h_t[:, 0:LANES] + h_t[:, LANES:pw]
            scale = lax.rsqrt(jnp.mean(hs_t * hs_t, axis=0, keepdims=True) + EPS)
            hs = (hs_t * scale).T
            for hh in range(2):
                vs = slice((2 * p + hh) * M_V_DIM, (2 * p + hh + 1) * M_V_DIM)
                y = hs[hh * lc:(hh + 1) * lc] * gmh_ref[:, vs] * og_ref[0, rows, vs].astype(F32)
                out_ref[0, rows, vs] = y.astype(BF16)
        return carry

    lax.fori_loop(0, nc, chunk, 0, unroll=4)


def _mlstm(mqd, mkd, mv, mo, gi2, gf2, gmh, *, seq):
    bsz = mqd.shape[0]
    nc = seq // M_CHUNK
    nrow = M_HEADS * nc
    assert 2 * M_CHUNK == LANES and 2 * M_QK_DIM == LANES and M_V_DIM == LANES
    b3 = lambda b: (b, 0, 0)
    slab = pltpu.VMEM((nrow, LANES), F32)
    return pl.pallas_call(
        functools.partial(_mlstm_kernel, seq=seq),
        grid=(bsz,),
        in_specs=[pl.BlockSpec((1, seq, M_QKW), b3),
                  pl.BlockSpec((1, seq, M_QKW), b3),
                  pl.BlockSpec((1, seq, M_W), b3),
                  pl.BlockSpec((1, seq, M_W), b3),
                  pl.BlockSpec((1, nrow, LANES), b3),
                  pl.BlockSpec((1, nrow, LANES), b3),
                  pl.BlockSpec((1, M_W), lambda b: (0, 0))],
        out_specs=pl.BlockSpec((1, seq, M_W), b3),
        out_shape=jax.ShapeDtypeStruct((bsz, seq, M_W), BF16),
        scratch_shapes=[slab, slab,
                        pltpu.VMEM((2, nrow, LANES), F32),
                        pltpu.VMEM((2, nrow, LANES), F32),
                        pltpu.VMEM((N_ROW_STATS, nrow, LANES), F32),
                        pltpu.VMEM((N_ROW_STATS, nrow, LANES), F32),
                        pltpu.VMEM((M_HEADS // 2 * nc * M_TILE, LANES), F32),
                        pltpu.VMEM((M_HEADS // 2 * nc * M_TILE, LANES), F32),
                        pltpu.VMEM((M_HEADS // 2, C_ROWS, 2 * LANES), F32),
                        pltpu.VMEM((M_HEADS // 2, nc, C_ROWS, 2 * LANES), BF16)],
        compiler_params=_cparams(1),
        name="mlstm",
    )(mqd, mkd, mv, mo, gi2, gf2, gmh)


ATT_QB = 128
ATT_SIDE = 64
ATT_KW = ATT_QB + 2 * ATT_SIDE
LOG2E = 1.4426950408889634
assert all(w // (2 * d) == ATT_SIDE for w, d in zip(WINDOWS, DILATIONS))
assert DILATIONS == (1, 4, 16)


def _band_bias(kw, offset):
    qi = lax.broadcasted_iota(jnp.int32, (2 * ATT_QB, kw), 0) % ATT_QB
    ki = lax.broadcasted_iota(jnp.int32, (2 * ATT_QB, kw), 1)
    return jnp.where(jnp.abs(ki + offset - qi) <= ATT_SIDE, 0.0, NEG_INF).astype(F32)


def _attn_block(qb, kb, vb, bias):
    first = lax.broadcasted_iota(jnp.int32, (1, LANES), 1) < A_HEAD_DIM
    zero = jnp.zeros_like(qb)
    qm = jnp.concatenate([jnp.where(first, qb, zero), jnp.where(first, zero, qb)], axis=0)
    s = lax.dot_general(qm, kb, (((1,), (1,)), ((), ())), preferred_element_type=F32) + bias
    m = jnp.max(s, axis=-1, keepdims=True)
    p = jnp.exp2(s - m)
    den = jnp.sum(p, axis=-1, keepdims=True)
    o = jnp.dot(p.astype(BF16), vb, preferred_element_type=F32) / den
    lse = m + jnp.log2(den)
    return (jnp.where(first, o[:ATT_QB], o[ATT_QB:]),
            jnp.where(first, lse[:ATT_QB], lse[ATT_QB:]))


def _merge(o_a, l_a, o_b, l_b):
    top = jnp.maximum(l_a, l_b)
    w_a = jnp.exp2(l_a - top)
    w_b = jnp.exp2(l_b - top)
    tot = w_a + w_b
    return (w_a * o_a + w_b * o_b) / tot, top + jnp.log2(tot)


def _dil_attn_kernel(q_ref, k_ref, v_ref, out_ref, tok_ref, r4f_ref, r4_ref, r16_ref,
                     o4_ref, l4_ref, ot_ref, lt_ref, bias_ref, *, seq):
    l4 = seq // 4
    l16 = seq // 16
    n_in = 3

    @pl.when((pl.program_id(0) == 0) & (pl.program_id(1) == 0))
    def _():
        bias_ref[0] = _band_bias(ATT_KW, 0)
        bias_ref[1] = _band_bias(ATT_KW, -ATT_SIDE)
        bias_ref[2] = _band_bias(ATT_KW, ATT_QB - ATT_KW)
        bias_ref[3, :, :ATT_QB] = _band_bias(ATT_QB, 0)

    for i, ref in enumerate((q_ref, k_ref, v_ref)):
        tok_ref[i] = ref[0].astype(F32)
    for i in range(n_in):
        for r in range(4):
            x = tok_ref[i, pl.ds(r, l4, stride=4), :]
            r4f_ref[i, r * l4:(r + 1) * l4, :] = x
            r4_ref[i, r * l4:(r + 1) * l4, :] = x.astype(BF16)
    for i in range(n_in):
        for c in range(16):
            x = r4f_ref[i, pl.ds((c // 4) * l4 + c % 4, l16, stride=4), :]
            r16_ref[i, c * l16:(c + 1) * l16, :] = x.astype(BF16)

    bias3 = bias_ref[3, :, :ATT_QB]

    def pat3(r4, carry):
        for r2 in range(4):
            q0 = _aligned((r4 * 4 + r2) * l16, l16)
            rows = pl.ds(q0, l16)
            o, l = _attn_block(r16_ref[0, rows, :], r16_ref[1, rows, :], r16_ref[2, rows, :], bias3)
            dst = pl.ds(r4 * l4 + r2, l16, stride=4)
            o4_ref[dst, :] = o
            l4_ref[dst, :] = l
        return carry

    lax.fori_loop(0, 4, pat3, 0, unroll=4)

    def window(ref, i, q0, jb, nb):
        off = 0 if jb == 0 else (ATT_QB - ATT_KW if jb == nb - 1 else -ATT_SIDE)
        return ref[i, pl.ds(_aligned(q0 + off, ATT_SIDE), ATT_KW), :]

    def bias_of(jb, nb):
        return bias_ref[0 if jb == 0 else (2 if jb == nb - 1 else 1)]

    nb4 = l4 // ATT_QB

    def pat2(r, carry):
        for jb in range(nb4):
            q0 = _aligned(r * l4 + jb * ATT_QB, ATT_QB)
            rows = pl.ds(q0, ATT_QB)
            o, l = _attn_block(r4_ref[0, rows, :], window(r4_ref, 1, q0, jb, nb4),
                               window(r4_ref, 2, q0, jb, nb4), bias_of(jb, nb4))
            o, l = _merge(o, l, o4_ref[rows, :], l4_ref[rows, :])
            dst = pl.ds(4 * jb * ATT_QB + r, ATT_QB, stride=4)
            ot_ref[dst, :] = o
            lt_ref[dst, :] = l
        return carry

    lax.fori_loop(0, 4, pat2, 0, unroll=4)

    nb1 = seq // ATT_QB

    def pat1_block(n, jb):
        q0 = _aligned(n * ATT_QB, ATT_QB)
        rows = pl.ds(q0, ATT_QB)
        o, l = _attn_block(q_ref[0, rows, :], window(k_ref, 0, q0, jb, nb1),
                           window(v_ref, 0, q0, jb, nb1), bias_of(jb, nb1))
        o, _ = _merge(o, l, ot_ref[rows, :], lt_ref[rows, :])
        out_ref[0, rows, :] = o.astype(BF16)

    per_iter = (nb1 - 2) // 2

    def pat1(g, carry):
        for u in range(per_iter):
            pat1_block(per_iter * g + 1 + u, 1)
        return carry

    pat1_block(0, 0)
    lax.fori_loop(0, 2, pat1, 0, unroll=2)
    for n in range(2 * per_iter + 1, nb1):
        pat1_block(n, n)


def _dil_attn(aq, ak, av, *, seq):
    bsz = aq.shape[0]
    n_pairs = A_W // LANES
    blk = pl.BlockSpec((1, seq, LANES), lambda b, p: (b, 0, p))
    f32_slab = pltpu.VMEM((seq, LANES), F32)
    return pl.pallas_call(
        functools.partial(_dil_attn_kernel, seq=seq),
        grid=(bsz, n_pairs),
        in_specs=[blk, blk, blk],
        out_specs=blk,
        out_shape=jax.ShapeDtypeStruct((bsz, seq, A_W), BF16),
        scratch_shapes=[pltpu.VMEM((3, seq, LANES), F32),
                        pltpu.VMEM((3, seq, LANES), F32),
                        pltpu.VMEM((3, seq, LANES), BF16),
                        pltpu.VMEM((3, seq, LANES), BF16),
                        f32_slab, f32_slab, f32_slab, f32_slab,
                        pltpu.VMEM((4, 2 * ATT_QB, ATT_KW), F32)],
        compiler_params=_cparams(2),
        name="dil_attn",
    )(aq, ak, av)


def _layer(x, mod3, g_ffn1, w_gu1, w_down1, g_mix, w_in, gate_bias, g_q, g_k, g_mh, w_out,
           g_ffn2, w_gu2, w_down2, g_final):
    bsz, seq, d = x.shape
    t = bsz * seq
    tm = 512
    row2 = lambda v: v.reshape(1, -1).astype(F32)

    o_aq = M_PROJ_W + N_GATES
    w_m = w_in[:, :M_PROJ_W].astype(BF16)
    w_g = jnp.pad(w_in[:, M_PROJ_W:o_aq], ((0, 0), (0, LANES - N_GATES))).astype(BF16)
    w_a = w_in[:, o_aq:].astype(BF16)
    gb = jnp.concatenate([gate_bias.reshape(1, N_GATES).astype(F32),
                          jnp.zeros((1, LANES - N_GATES), F32)], axis=1)
    gq = jnp.tile(row2(g_q), (1, A_HEADS))
    gk = jnp.tile(row2(g_k), (1, A_HEADS))
    x1, mq, mk, mv, mo, gates, aq, ak, av = _ffn(
        x.reshape(t, d), mod3, row2(g_ffn1), w_gu1.astype(F32), w_down1.astype(F32),
        seq=seq, mod_base=0, tm=tm,
        proj=(row2(g_mix), w_m, w_g, w_a, gb, gq, gk, _rope_tables(seq)))

    to3 = lambda a: a.reshape(bsz, seq, a.shape[-1])
    nc = seq // M_CHUNK
    g5 = gates.reshape(2, 2, M_HEADS, bsz, nc, M_CHUNK)
    g5 = g5.transpose(1, 3, 2, 4, 0, 5).reshape(2, bsz, M_HEADS * nc, 2 * M_CHUNK)
    hm = _mlstm(to3(mq), to3(mk), to3(mv), to3(mo), g5[0], g5[1], row2(g_mh), seq=seq)
    ha = _dil_attn(to3(aq), to3(ak), to3(av), seq=seq)

    out = _ffn(x1, mod3, row2(g_ffn2), w_gu2.astype(F32), w_down2.astype(F32),
               seq=seq, mod_base=6, tm=tm,
               mix=(hm.reshape(t, M_W), ha.reshape(t, A_W), w_out.astype(F32)),
               gfin=row2(g_final))
    return out.reshape(bsz, seq, d)


def kernel(x, c, w_ada, b_ada, g_ffn1, w_gu1, w_down1, g_mix, w_in, gate_bias, g_q, g_k, g_mh,
           w_out, g_ffn2, w_gu2, w_down2, g_final):
    bsz, seq, d = x.shape
    depth = w_ada.shape[0]
    for l in range(depth):
        mod = _adaln(c, w_ada[l], b_ada[l].reshape(1, -1))
        mod3 = mod.reshape(bsz, N_MOD, d)
        x = _layer(x, mod3, g_ffn1[l], w_gu1[l], w_down1[l], g_mix[l], w_in[l], gate_bias[l],
                   g_q[l], g_k[l], g_mh[l], w_out[l], g_ffn2[l], w_gu2[l], w_down2[l], g_final[l])
    return x
```
